```python
import math, functools
import jax, jax.numpy as jnp
from jax import lax
import numpy as np

D_MODEL = 2048
BATCH = 4
SEQ = 8192
DEPTH = 4

GRID_W = 64
CTX_LEN = 256
D_CONV = 1024
CONV_W = 31
CONV_PAD = CONV_W // 2
NA_HEADS = 8
NA_DH = 128
D_ATT = NA_HEADS * NA_DH
NA_KH = 8
NA_KW = 16
OFF_Q = 2 * D_CONV
OFF_K = OFF_Q + D_ATT
OFF_V = OFF_K + D_ATT
OFF_GATE = OFF_V + D_ATT
D_IN = OFF_GATE + 2 * D_MODEL
P_HEADS = 8
N_KEYS = 128
N_EXPERTS = N_KEYS * N_KEYS
D_KEY = 256
D_KEY_HALF = D_KEY // 2
PK_TOPK = 16
PEER_CHUNK = 128
EPS = 1e-6

kernel_name = "hybrid_conformer_natten_peer_dit"


def rms_norm(h, g):
    h32 = h.astype(jnp.float32)
    n = h32 * lax.rsqrt(jnp.mean(h32 * h32, axis=-1, keepdims=True) + EPS)
    return n.astype(h.dtype) * g


def unit_rms(h):
    h32 = h.astype(jnp.float32)
    return (h32 * lax.rsqrt(jnp.mean(h32 * h32, axis=-1, keepdims=True) + EPS)).astype(h.dtype)


def layer_norm(h, g, b):
    h32 = h.astype(jnp.float32)
    mu = jnp.mean(h32, axis=-1, keepdims=True)
    var = jnp.mean(jnp.square(h32 - mu), axis=-1, keepdims=True)
    return ((h32 - mu) * lax.rsqrt(var + EPS)).astype(h.dtype) * g + b


def modulate(h, shift, scale):
    return h * (1.0 + scale) + shift


def heads(t):
    return t.reshape(t.shape[:-1] + (NA_HEADS, NA_DH))


def conformer_conv(glu_in, dw_w, dw_b, ln_g, ln_b, w_out):
    a, gt = jnp.split(glu_in, 2, axis=-1)
    h = a * jax.nn.sigmoid(gt)
    h = lax.conv_general_dilated(h, dw_w[:, None, :], window_strides=(1,), padding=[(CONV_PAD, CONV_PAD)],
                                 dimension_numbers=("NWC", "WIO", "NWC"), feature_group_count=D_CONV) + dw_b
    h = jax.nn.silu(layer_norm(h, ln_g, ln_b))
    return h @ w_out


def context_attention(qc, kc, vc):
    s = jnp.einsum("blhd,bmhd->bhlm", qc, kc).astype(jnp.float32) * (NA_DH ** -0.5)
    p = jax.nn.softmax(s, axis=-1).astype(vc.dtype)
    o = jnp.einsum("bhlm,bmhd->blhd", p, vc)
    return o.reshape(o.shape[:2] + (D_ATT,))


def neighborhood_attention(q, k, v, kc, vc, rpb_l, rows, kh):
    B, T = q.shape[0], q.shape[1]
    qg = q.reshape(B, rows, GRID_W, NA_HEADS, NA_DH)
    kg = k.reshape(B, rows, GRID_W, NA_HEADS, NA_DH)
    vg = v.reshape(B, rows, GRID_W, NA_HEADS, NA_DH)
    cols = jnp.arange(GRID_W)
    col_start = jnp.clip(cols - NA_KW // 2, 0, GRID_W - NA_KW)
    col_idx = col_start[:, None] + jnp.arange(NA_KW)[None, :]
    dc = col_idx - cols[:, None]
    n_win = kh * NA_KW
    scale = NA_DH ** -0.5

    def row_block(r):
        rs = jnp.clip(r - kh // 2, 0, rows - kh)
        q_r = lax.dynamic_index_in_dim(qg, r, axis=1, keepdims=False)
        k_rows = lax.dynamic_slice_in_dim(kg, rs, kh, axis=1)
        v_rows = lax.dynamic_slice_in_dim(vg, rs, kh, axis=1)
        k_win = k_rows[:, :, col_idx]
        v_win = v_rows[:, :, col_idx]
        dr = rs + jnp.arange(kh) - r
        bias = rpb_l[:, dr[:, None, None] + (NA_KH - 1), dc[None] + (NA_KW - 1)]
        bias = jnp.transpose(bias, (0, 2, 1, 3))[None]
        s_win = jnp.einsum("bwhd,biwjhd->bhwij", q_r, k_win).astype(jnp.float32) * scale + bias
        s_ctx = jnp.einsum("bwhd,blhd->bhwl", q_r, kc).astype(jnp.float32) * scale
        s = jnp.concatenate([s_win.reshape(B, NA_HEADS, GRID_W, n_win), s_ctx], axis=-1)
        p = jax.nn.softmax(s, axis=-1).astype(v.dtype)
        p_win = p[..., :n_win].reshape(B, NA_HEADS, GRID_W, kh, NA_KW)
        p_ctx = p[..., n_win:]
        return (jnp.einsum("bhwij,biwjhd->bwhd", p_win, v_win)
                + jnp.einsum("bhwl,blhd->bwhd", p_ctx, vc))

    o = lax.map(row_block, jnp.arange(rows))
    return jnp.transpose(o, (1, 0, 2, 3, 4)).reshape(B, T, D_ATT)


def peer(h, w_pq, sub_keys, w_down, w_up):
    B, L, D = h.shape
    flat = h.reshape(-1, PEER_CHUNK, D)

    def chunk(xc):
        q = unit_rms((xc @ w_pq).reshape(PEER_CHUNK, P_HEADS, 2, D_KEY_HALF))
        s = jnp.einsum("chpd,hpkd->chpk", q, sub_keys).astype(jnp.float32)
        s1, i1 = lax.top_k(s[:, :, 0], PK_TOPK)
        s2, i2 = lax.top_k(s[:, :, 1], PK_TOPK)
        cand = (s1[..., :, None] + s2[..., None, :]).reshape(PEER_CHUNK, P_HEADS, PK_TOPK * PK_TOPK)
        cidx = (i1[..., :, None] * N_KEYS + i2[..., None, :]).reshape(PEER_CHUNK, P_HEADS, PK_TOPK * PK_TOPK)
        sc, pos = lax.top_k(cand, PK_TOPK)
        ids = jnp.take_along_axis(cidx, pos, axis=-1)
        g = jax.nn.softmax(sc, axis=-1).astype(xc.dtype)
        u = w_down[ids]
        act = jax.nn.gelu(jnp.einsum("cd,chkd->chk", xc, u), approximate=False)
        return jnp.einsum("chk,chkd->cd", g * act, w_up[ids])

    return lax.map(chunk, flat).reshape(B, L, D)


def setup_inputs(seed: int = 0) -> dict:
    key = jax.random.key(seed)
    ks = jax.random.split(key, 24)
    D = D_MODEL
    L = DEPTH

    def nrm(k, shape, s):
        return jax.random.normal(k, shape, jnp.float32) * s

    return {
        "x": nrm(ks[0], (BATCH, SEQ, D), 1.0),
        "c": nrm(ks[1], (BATCH, D), 1.0),
        "ctx": nrm(ks[2], (BATCH, CTX_LEN, D), 1.0),
        "c_ctx": nrm(ks[3], (D,), 1.0),
        "w_mod": nrm(ks[4], (L, D, 6 * D), 0.5 * D ** -0.5),
        "b_mod": nrm(ks[5], (L, 6 * D), 0.01),
        "norm1_g": 1.0 + nrm(ks[6], (L, D), 0.02),
        "w_in": nrm(ks[7], (L, D, D_IN), D ** -0.5),
        "dw_w": nrm(ks[8], (L, CONV_W, D_CONV), CONV_W ** -0.5),
        "dw_b": nrm(ks[9], (L, D_CONV), 0.01),
        "cln_g": 1.0 + nrm(ks[10], (L, D_CONV), 0.02),
        "cln_b": nrm(ks[11], (L, D_CONV), 0.01),
        "w_conv_out": nrm(ks[12], (L, D_CONV, D), D_CONV ** -0.5),
        "qn_g": 1.0 + nrm(ks[13], (L, NA_DH), 0.02),
        "kn_g": 1.0 + nrm(ks[14], (L, NA_DH), 0.02),
        "rpb": nrm(ks[15], (L, NA_HEADS, 2 * NA_KH - 1, 2 * NA_KW - 1), 0.1),
        "w_na_out": nrm(ks[16], (L, D_ATT, D), D_ATT ** -0.5),
        "w_o": nrm(ks[17], (L, D, D), D ** -0.5),
        "norm2_g": 1.0 + nrm(ks[18], (L, D), 0.02),
        "w_pq": nrm(ks[19], (L, D, P_HEADS * D_KEY), D ** -0.5),
        "sub_keys": nrm(ks[20], (L, P_HEADS, 2, N_KEYS, D_KEY_HALF), D_KEY_HALF ** -0.5),
        "w_down": nrm(ks[21], (L, N_EXPERTS, D), D ** -0.5),
        "w_up": nrm(ks[22], (L, N_EXPERTS, D), P_HEADS ** -0.5),
    }


def reference(x, c, ctx, c_ctx, w_mod, b_mod, norm1_g, w_in, dw_w, dw_b, cln_g, cln_b, w_conv_out,
              qn_g, kn_g, rpb, w_na_out, w_o, norm2_g, w_pq, sub_keys, w_down, w_up):
    B, T, D = x.shape
    rows = T // GRID_W
    kh = min(NA_KH, rows)
    silu_c = jax.nn.silu(c)
    silu_cc = jax.nn.silu(c_ctx)
    cx = ctx
    for l in range(DEPTH):
        last = l == DEPTH - 1
        mod_l = (silu_c @ w_mod[l] + b_mod[l])[:, None, :]
        mod_c = (silu_cc @ w_mod[l] + b_mod[l])[None, None, :]
        sh1, sc1, g1, sh2, sc2, g2 = jnp.split(mod_l, 6, axis=-1)
        csh1, csc1, cg1, csh2, csc2, cg2 = jnp.split(mod_c, 6, axis=-1)

        hl = modulate(rms_norm(x, norm1_g[l]), sh1, sc1)
        hc = modulate(rms_norm(cx, norm1_g[l]), csh1, csc1)
        pl = hl @ w_in[l]
        glu_l = pl[..., :OFF_Q]
        q_l = rms_norm(heads(pl[..., OFF_Q:OFF_K]), qn_g[l])
        k_l = rms_norm(heads(pl[..., OFF_K:OFF_V]), kn_g[l])
        v_l = heads(pl[..., OFF_V:OFF_GATE])
        gate_conv_l = pl[..., OFF_GATE:OFF_GATE + D]
        gate_na_l = pl[..., OFF_GATE + D:]
        if last:
            pc = hc @ w_in[l][:, OFF_K:OFF_GATE]
            k_c = rms_norm(heads(pc[..., :D_ATT]), kn_g[l])
            v_c = heads(pc[..., D_ATT:])
        else:
            pc = hc @ w_in[l]
            k_c = rms_norm(heads(pc[..., OFF_K:OFF_V]), kn_g[l])
            v_c = heads(pc[..., OFF_V:OFF_GATE])

        na_l = neighborhood_attention(q_l, k_l, v_l, k_c, v_c, rpb[l], rows, kh) @ w_na_out[l]
        conv_l = conformer_conv(glu_l, dw_w[l], dw_b[l], cln_g[l], cln_b[l], w_conv_out[l])
        y_l = jax.nn.sigmoid(gate_conv_l) * conv_l + jax.nn.sigmoid(gate_na_l) * na_l
        x = x + g1 * (y_l @ w_o[l])

        if not last:
            q_c = rms_norm(heads(pc[..., OFF_Q:OFF_K]), qn_g[l])
            na_c = context_attention(q_c, k_c, v_c) @ w_na_out[l]
            conv_c = conformer_conv(pc[..., :OFF_Q], dw_w[l], dw_b[l], cln_g[l], cln_b[l], w_conv_out[l])
            y_c = (jax.nn.sigmoid(pc[..., OFF_GATE:OFF_GATE + D]) * conv_c
                   + jax.nn.sigmoid(pc[..., OFF_GATE + D:]) * na_c)
            cx = cx + cg1 * (y_c @ w_o[l])

        h2 = modulate(rms_norm(x, norm2_g[l]), sh2, sc2)
        x = x + g2 * peer(h2, w_pq[l], sub_keys[l], w_down[l], w_up[l])
        if not last:
            h2c = modulate(rms_norm(cx, norm2_g[l]), csh2, csc2)
            cx = cx + cg2 * peer(h2c, w_pq[l], sub_keys[l], w_down[l], w_up[l])
    return x
```

```python
import functools
import math

import numpy as np
import jax
import jax.numpy as jnp
from jax import lax
from jax.experimental import pallas as pl
from jax.experimental.pallas import tpu as pltpu

F32 = jnp.float32
BF16 = jnp.bfloat16

GRID_W = 64
D_CONV = 1024
CONV_W = 31
CONV_PAD = CONV_W // 2
NA_HEADS = 8
NA_DH = 128
D_ATT = NA_HEADS * NA_DH
NA_KH = 8
NA_KW = 16
P_HEADS = 8
N_KEYS = 128
D_KEY_HALF = 128
PK_TOPK = 16
EPS = 1e-6

LANES = 128
HALO = 16
NA_ROWS_PER_STEP = 4
NA_BAND_BLOCKS = 3
MASK_VALUE = -1e30
VMEM_LIMIT = 56 * 1024 * 1024

NT_DIMS = (((1,), (1,)), ((), ()))


def _params(sem, vmem=VMEM_LIMIT):
    return pltpu.CompilerParams(dimension_semantics=sem, vmem_limit_bytes=vmem)


def _sigmoid(x):
    return 1.0 / (1.0 + jnp.exp(-x))


def _mod_kernel(c_ref, w_ref, b_ref, o_ref):
    c = c_ref[...]
    s = (c * _sigmoid(c)).astype(BF16)
    o_ref[0] = jnp.dot(s, w_ref[0].astype(BF16), preferred_element_type=F32) + b_ref[0]


def _mod_call(cvec, w_mod, b_mod):
    nl, d, n = w_mod.shape
    r = cvec.shape[0]
    tn = 512
    return pl.pallas_call(
        _mod_kernel,
        grid=(nl, n // tn),
        in_specs=[pl.BlockSpec((r, d), lambda l, j: (0, 0)),
                  pl.BlockSpec((1, d, tn), lambda l, j: (l, 0, j)),
                  pl.BlockSpec((1, 1, tn), lambda l, j: (l, 0, j))],
        out_specs=pl.BlockSpec((1, r, tn), lambda l, j: (l, 0, j)),
        out_shape=jax.ShapeDtypeStruct((nl, r, n), F32),
        compiler_params=_params(("parallel", "parallel")),
        name="adaln_mod",
    )(cvec, w_mod, b_mod)


def _normmod_kernel(x_ref, g_ref, sh_ref, sc_ref, o_ref):
    x = x_ref[0]
    ms = jnp.mean(x * x, axis=-1, keepdims=True)
    n = x * lax.rsqrt(ms + EPS) * g_ref[...]
    o_ref[0] = (n * (1.0 + sc_ref[0]) + sh_ref[0]).astype(o_ref.dtype)


def _normmod_call(x, g, shift, scale):
    b, l, d = x.shape
    tl = min(l, 512)
    return pl.pallas_call(
        _normmod_kernel,
        grid=(b, l // tl),
        in_specs=[pl.BlockSpec((1, tl, d), lambda i, j: (i, j, 0)),
                  pl.BlockSpec((1, d), lambda i, j: (0, 0)),
                  pl.BlockSpec((1, 1, d), lambda i, j: (i, 0, 0)),
                  pl.BlockSpec((1, 1, d), lambda i, j: (i, 0, 0))],
        out_specs=pl.BlockSpec((1, tl, d), lambda i, j: (i, j, 0)),
        out_shape=jax.ShapeDtypeStruct((b, l, d), BF16),
        compiler_params=_params(("parallel", "parallel")),
        name="normmod",
    )(x, g, shift, scale)


def _mm_glu_kernel(h_ref, wa_ref, wg_ref, o_ref):
    h = h_ref[...]
    a = jnp.dot(h, wa_ref[...], preferred_element_type=F32)
    g = jnp.dot(h, wg_ref[...], preferred_element_type=F32)
    o_ref[...] = (a * _sigmoid(g)).astype(o_ref.dtype)


def _mm_headnorm_kernel(h_ref, w_ref, g_ref, o_ref):
    y = jnp.dot(h_ref[...], w_ref[...], preferred_element_type=F32)
    for k in range(y.shape[1] // LANES):
        sl = slice(k * LANES, (k + 1) * LANES)
        yk = y[:, sl]
        ms = jnp.mean(yk * yk, axis=-1, keepdims=True)
        o_ref[:, sl] = (yk * lax.rsqrt(ms + EPS) * g_ref[:, sl]).astype(o_ref.dtype)


def _mm_plain_kernel(h_ref, w_ref, o_ref):
    o_ref[...] = jnp.dot(h_ref[...], w_ref[...], preferred_element_type=F32).astype(o_ref.dtype)


def _mm_sigmoid_kernel(h_ref, w_ref, o_ref):
    y = jnp.dot(h_ref[...], w_ref[...], preferred_element_type=F32)
    o_ref[...] = _sigmoid(y).astype(o_ref.dtype)


def _mm_call(kernel, h, ws, extras, n_out, out_dtype, name):
    t, k = h.shape
    tm = min(t, 1024)
    tn = 512
    in_specs = [pl.BlockSpec((tm, k), lambda i, j: (i, 0))]
    in_specs += [pl.BlockSpec((k, tn), lambda i, j: (0, j)) for _ in ws]
    in_specs += [pl.BlockSpec((1, tn), lambda i, j: (0, j)) for _ in extras]
    return pl.pallas_call(
        kernel,
        grid=(t // tm, n_out // tn),
        in_specs=in_specs,
        out_specs=pl.BlockSpec((tm, tn), lambda i, j: (i, j)),
        out_shape=jax.ShapeDtypeStruct((t, n_out), out_dtype),
        compiler_params=_params(("parallel", "parallel")),
        name=name,
    )(h, *ws, *extras)


def _softmax_pv(s_list, v_list):
    m = s_list[0].max(axis=-1, keepdims=True)
    for s in s_list[1:]:
        m = jnp.maximum(m, s.max(axis=-1, keepdims=True))
    num = None
    den = None
    for s, v in zip(s_list, v_list):
        p = jnp.exp(s - m)
        d = p.sum(axis=-1, keepdims=True)
        o = jnp.dot(p.astype(BF16), v, preferred_element_type=F32)
        num = o if num is None else num + o
        den = d if den is None else den + d
    return num / den


def _na_kernel(q_ref, k0_ref, k1_ref, k2_ref, v0_ref, v1_ref, v2_ref, kc_ref, vc_ref, bias_ref, o_ref):
    scale = NA_DH ** -0.5
    for h in range(NA_HEADS):
        sl = slice(h * NA_DH, (h + 1) * NA_DH)
        q = q_ref[0, :, sl]
        kb = jnp.concatenate([k0_ref[0, :, sl], k1_ref[0, :, sl], k2_ref[0, :, sl]], axis=0)
        vb = jnp.concatenate([v0_ref[0, :, sl], v1_ref[0, :, sl], v2_ref[0, :, sl]], axis=0)
        s_w = lax.dot_general(q, kb, NT_DIMS, preferred_element_type=F32) * scale + bias_ref[0, h]
        s_c = lax.dot_general(q, kc_ref[0, :, sl], NT_DIMS, preferred_element_type=F32) * scale
        o = _softmax_pv([s_w, s_c], [vb, vc_ref[0, :, sl]])
        o_ref[0, :, sl] = o.astype(o_ref.dtype)


def _na_bias_table(rpb_l, rows):
    r4 = NA_ROWS_PER_STEP
    band_rows = NA_BAND_BLOCKS * r4
    qr = np.arange(r4)[:, None, None, None]
    qc = np.arange(GRID_W)[None, :, None, None]
    kr = np.arange(band_rows)[None, None, :, None]
    kc = np.arange(GRID_W)[None, None, None, :]
    cs = np.clip(qc - NA_KW // 2, 0, GRID_W - NA_KW)
    col_ok = (kc >= cs) & (kc < cs + NA_KW)
    dc = kc - qc
    tables = []
    for off, ks in ((0, np.zeros_like(qr)), (r4, qr), (band_rows - r4, np.full_like(qr, band_rows - NA_KH))):
        row_ok = (kr >= ks) & (kr < ks + NA_KH)
        dr = kr - off - qr
        ok = np.broadcast_to(row_ok & col_ok, (r4, GRID_W, band_rows, GRID_W))
        dri = np.broadcast_to(np.clip(dr + NA_KH - 1, 0, 2 * NA_KH - 2), ok.shape)
        dci = np.broadcast_to(np.clip(dc + NA_KW - 1, 0, 2 * NA_KW - 2), ok.shape)
        n_q, n_k = r4 * GRID_W, band_rows * GRID_W
        bias = rpb_l[:, dri.reshape(n_q, n_k), dci.reshape(n_q, n_k)]
        tables.append(jnp.where(ok.reshape(n_q, n_k)[None], bias, MASK_VALUE))
    return jnp.stack(tables, axis=0)


def _na_call(q, k, v, kc, vc, bias):
    b, t, _ = q.shape
    c = kc.shape[1]
    tq = NA_ROWS_PER_STEP * GRID_W
    nb = t // tq
    assert nb >= NA_BAND_BLOCKS + 1 and NA_KH == 2 * NA_ROWS_PER_STEP

    def band(d):
        return pl.BlockSpec((1, tq, D_ATT), lambda i, j: (i, jnp.clip(j - 1, 0, nb - NA_BAND_BLOCKS) + d, 0))

    def variant(i, j):
        return (jnp.where(j == 0, 0, jnp.where(j == nb - 1, 2, 1)), 0, 0, 0)

    return pl.pallas_call(
        _na_kernel,
        grid=(b, nb),
        in_specs=[pl.BlockSpec((1, tq, D_ATT), lambda i, j: (i, j, 0)),
                  band(0), band(1), band(2), band(0), band(1), band(2),
                  pl.BlockSpec((1, c, D_ATT), lambda i, j: (i, 0, 0)),
                  pl.BlockSpec((1, c, D_ATT), lambda i, j: (i, 0, 0)),
                  pl.BlockSpec((1, NA_HEADS, tq, NA_BAND_BLOCKS * tq), variant)],
        out_specs=pl.BlockSpec((1, tq, D_ATT), lambda i, j: (i, j, 0)),
        out_shape=jax.ShapeDtypeStruct((b, t, D_ATT), BF16),
        compiler_params=_params(("parallel", "arbitrary")),
        name="neighborhood_attention",
    )(q, k, k, k, v, v, v, kc, vc, bias)


def _ctx_attn_kernel(q_ref, k_ref, v_ref, o_ref):
    scale = NA_DH ** -0.5
    for h in range(NA_HEADS):
        sl = slice(h * NA_DH, (h + 1) * NA_DH)
        s = lax.dot_general(q_ref[0, :, sl], k_ref[0, :, sl], NT_DIMS, preferred_element_type=F32) * scale
        o_ref[0, :, sl] = _softmax_pv([s], [v_ref[0, :, sl]]).astype(o_ref.dtype)


def _ctx_attn_call(q, k, v):
    b, c, _ = q.shape
    spec = pl.BlockSpec((1, c, D_ATT), lambda i: (i, 0, 0))
    return pl.pallas_call(
        _ctx_attn_kernel,
        grid=(b,),
        in_specs=[spec, spec, spec],
        out_specs=spec,
        out_shape=jax.ShapeDtypeStruct((b, c, D_ATT), BF16),
        compiler_params=_params(("parallel",)),
        name="context_attention",
    )(q, k, v)


def _conv_kernel(prev_ref, cur_ref, next_ref, w_ref, b_ref, g_ref, bb_ref, o_ref, buf_ref):
    i = pl.program_id(1)
    tl = cur_ref.shape[1]
    buf_ref[0:HALO] = jnp.where(i > 0, prev_ref[0], 0.0)
    buf_ref[HALO:HALO + tl] = cur_ref[0]
    buf_ref[HALO + tl:2 * HALO + tl] = jnp.where(i < pl.num_programs(1) - 1, next_ref[0], 0.0)
    rows = 32
    base = HALO - CONV_PAD

    for r0 in range(0, tl, rows):
        acc = jnp.zeros((rows, D_CONV), F32) + b_ref[...]
        for k in range(CONV_W):
            acc = acc + buf_ref[r0 + base + k:r0 + base + k + rows, :] * w_ref[k:k + 1, :]
        mu = jnp.mean(acc, axis=-1, keepdims=True)
        cen = acc - mu
        var = jnp.mean(cen * cen, axis=-1, keepdims=True)
        y = cen * lax.rsqrt(var + EPS) * g_ref[...] + bb_ref[...]
        o_ref[0, r0:r0 + rows, :] = (y * _sigmoid(y)).astype(o_ref.dtype)


def _conv_call(glu, dw_w, dw_b, ln_g, ln_b):
    b, l, c = glu.shape
    tl = min(l, 128)
    hb = tl // HALO
    last_halo = l // HALO - 1
    vec = pl.BlockSpec((1, c), lambda i, j: (0, 0))
    return pl.pallas_call(
        _conv_kernel,
        grid=(b, l // tl),
        in_specs=[pl.BlockSpec((1, HALO, c), lambda i, j: (i, jnp.maximum(j * hb - 1, 0), 0)),
                  pl.BlockSpec((1, tl, c), lambda i, j: (i, j, 0)),
                  pl.BlockSpec((1, HALO, c), lambda i, j: (i, jnp.minimum((j + 1) * hb, last_halo), 0)),
                  pl.BlockSpec((CONV_W, c), lambda i, j: (0, 0)),
                  vec, vec, vec],
        out_specs=pl.BlockSpec((1, tl, c), lambda i, j: (i, j, 0)),
        out_shape=jax.ShapeDtypeStruct((b, l, c), BF16),
        scratch_shapes=[pltpu.VMEM((tl + 2 * HALO, c), F32)],
        compiler_params=_params(("parallel", "arbitrary")),
        name="conformer_conv",
    )(glu, glu, glu, dw_w, dw_b, ln_g, ln_b)


def _merge_kernel(cp_ref, na_ref, sgc_ref, sgn_ref, wc_ref, wn_ref, y_ref):
    conv = jnp.dot(cp_ref[...], wc_ref[...], preferred_element_type=F32)
    na = jnp.dot(na_ref[...], wn_ref[...], preferred_element_type=F32)
    y_ref[...] = (sgc_ref[...] * conv + sgn_ref[...] * na).astype(y_ref.dtype)


def _merge_call(cpre, na, sg, w_conv_out, w_na_out):
    t, kc = cpre.shape
    d = w_conv_out.shape[1]
    tm = min(t, 512)
    return pl.pallas_call(
        _merge_kernel,
        grid=(t // tm,),
        in_specs=[pl.BlockSpec((tm, kc), lambda i: (i, 0)),
                  pl.BlockSpec((tm, kc), lambda i: (i, 0)),
                  pl.BlockSpec((tm, d), lambda i: (i, 0)),
                  pl.BlockSpec((tm, d), lambda i: (i, 1)),
                  pl.BlockSpec((kc, d), lambda i: (0, 0)),
                  pl.BlockSpec((kc, d), lambda i: (0, 0))],
        out_specs=pl.BlockSpec((tm, d), lambda i: (i, 0)),
        out_shape=jax.ShapeDtypeStruct((t, d), BF16),
        compiler_params=_params(("parallel",)),
        name="branch_merge",
    )(cpre, na, sg, sg, w_conv_out, w_na_out)


def _oproj_kernel(y_ref, w_ref, x_ref, g_ref, o_ref):
    o_ref[0] = x_ref[0] + g_ref[0] * jnp.dot(y_ref[0], w_ref[...], preferred_element_type=F32)


def _oproj_call(y, w_o, x, gate):
    b, l, d = x.shape
    tm = min(l, 512)
    return pl.pallas_call(
        _oproj_kernel,
        grid=(b, l // tm),
        in_specs=[pl.BlockSpec((1, tm, d), lambda i, j: (i, j, 0)),
                  pl.BlockSpec((d, d), lambda i, j: (0, 0)),
                  pl.BlockSpec((1, tm, d), lambda i, j: (i, j, 0)),
                  pl.BlockSpec((1, 1, d), lambda i, j: (i, 0, 0))],
        out_specs=pl.BlockSpec((1, tm, d), lambda i, j: (i, j, 0)),
        out_shape=jax.ShapeDtypeStruct((b, l, d), F32),
        compiler_params=_params(("parallel", "parallel")),
        name="out_proj_residual",
    )(y, w_o, x, gate)


def _peer_scores_kernel(h_ref, w_ref, sk_ref, o_ref):
    q = jnp.dot(h_ref[...], w_ref[...], preferred_element_type=F32)
    for g in range(2 * P_HEADS):
        qg = q[:, g * D_KEY_HALF:(g + 1) * D_KEY_HALF]
        ms = jnp.mean(qg * qg, axis=-1, keepdims=True)
        qn = (qg * lax.rsqrt(ms + EPS)).astype(BF16)
        o_ref[g] = lax.dot_general(sk_ref[g], qn, NT_DIMS, preferred_element_type=F32)


def _peer_scores_call(h, w_pq, sub_keys):
    t, d = h.shape
    ng = 2 * P_HEADS
    tm = min(t, 256)
    return pl.pallas_call(
        _peer_scores_kernel,
        grid=(t // tm,),
        in_specs=[pl.BlockSpec((tm, d), lambda i: (i, 0)),
                  pl.BlockSpec((d, ng * D_KEY_HALF), lambda i: (0, 0)),
                  pl.BlockSpec((ng, N_KEYS, D_KEY_HALF), lambda i: (0, 0, 0))],
        out_specs=pl.BlockSpec((ng, N_KEYS, tm), lambda i: (0, 0, i)),
        out_shape=jax.ShapeDtypeStruct((ng, N_KEYS, t), F32),
        compiler_params=_params(("parallel",)),
        name="peer_scores",
    )(h, w_pq, sub_keys)


def _top16_rows(s, iota):
    rank = jnp.full(s.shape, float(PK_TOPK), F32)
    vals = []
    for k in range(PK_TOPK):
        m = jnp.max(s, axis=0, keepdims=True)
        idx = jnp.min(jnp.where(s == m, iota, float(N_KEYS)), axis=0, keepdims=True)
        sel = iota == idx
        rank = jnp.where(sel, float(k), rank)
        s = jnp.where(sel, -jnp.inf, s)
        vals.append(m)
    return vals, rank


def _peer_topk_kernel(s_ref, rank2_ref, cnt_ref, e2_ref, e1_ref):
    width = LANES
    n_chunks = s_ref.shape[2] // width
    n_cand = 80
    r = lax.broadcasted_iota(jnp.int32, (n_cand, width), 0)
    ci = jnp.where(r < 16, 0, jnp.where(r < 72, 1 + ((r - 16) >> 3), r - 64))
    cj = jnp.where(r < 16, r, jnp.where(r < 72, (r - 16) & 7, 0))
    valid = (ci + 1) * (cj + 1) <= PK_TOPK
    pos = (ci * PK_TOPK + cj).astype(F32)
    iota = lax.broadcasted_iota(jnp.int32, (N_KEYS, width), 0).astype(F32)

    def chunk(c, carry):
        ls = pl.ds(pl.multiple_of(c * width, width), width)
        s1 = s_ref[0, :, ls]
        s2 = s_ref[1, :, ls]
        v1, rank1 = _top16_rows(s1, iota)
        v2, rank2 = _top16_rows(s2, iota)
        v2_all = jnp.concatenate(v2, axis=0)
        blocks = [v1[0] + v2_all]
        for i in range(1, 8):
            blocks.append(v1[i] + v2_all[0:8])
        blocks.append(jnp.concatenate(v1[8:], axis=0) + v2[0])
        cand = jnp.where(valid, jnp.concatenate(blocks, axis=0), -jnp.inf)
        selm = jnp.zeros((n_cand, width), F32)
        top = None
        z = None
        for k in range(PK_TOPK):
            m = jnp.max(cand, axis=0, keepdims=True)
            pmin = jnp.min(jnp.where(cand == m, pos, 1e9), axis=0, keepdims=True)
            sel = pos == pmin
            selm = jnp.where(sel, 1.0, selm)
            cand = jnp.where(sel, -jnp.inf, cand)
            if k == 0:
                top = m
                z = jnp.ones_like(m)
            else:
                z = z + jnp.exp(m - top)
        counts = [jnp.sum(selm[0:16], axis=0, keepdims=True)]
        for i in range(1, 8):
            counts.append(jnp.sum(selm[8 + 8 * i:16 + 8 * i], axis=0, keepdims=True))
        for q in range(8):
            counts.append(selm[72 + q:73 + q])
        cnt = jnp.zeros((N_KEYS, width), F32)
        for i in range(PK_TOPK):
            cnt = jnp.where(rank1 == float(i), counts[i], cnt)
        rank2_ref[0, :, ls] = rank2
        cnt_ref[0, :, ls] = cnt
        e2_ref[0, :, ls] = jnp.exp(s2 - v2[0])
        e1_ref[0, :, ls] = jnp.exp(s1 - v1[0]) / z
        return carry

    lax.fori_loop(0, n_chunks, chunk, 0)


def _peer_topk_call(scores):
    ng, nk, t = scores.shape
    tt = min(t, 512)
    out = jax.ShapeDtypeStruct((P_HEADS, nk, t), F32)
    ospec = pl.BlockSpec((1, nk, tt), lambda i, h: (h, 0, i))
    return pl.pallas_call(
        _peer_topk_kernel,
        grid=(t // tt, P_HEADS),
        in_specs=[pl.BlockSpec((2, nk, tt), lambda i, h: (h, 0, i))],
        out_specs=[ospec, ospec, ospec, ospec],
        out_shape=[out, out, out, out],
        compiler_params=_params(("parallel", "parallel")),
        name="peer_topk",
    )(scores)


def _peer_dense_kernel(h_ref, rank2_ref, cnt_ref, e2_ref, e1_ref, wd_ref, wu_ref, x_ref, g_ref, o_ref, acc_ref):
    j = pl.program_id(1)
    ce = wd_ref.shape[0]
    per_step = ce // N_KEYS

    @pl.when(j == 0)
    def _():
        acc_ref[...] = jnp.zeros_like(acc_ref)

    a_t = lax.dot_general(wd_ref[...], h_ref[...], NT_DIMS, preferred_element_type=F32)
    act = a_t * (lax.erf(a_t * (1.0 / math.sqrt(2.0))) + 1.0) * 0.5
    gates = []
    for ai in range(per_step):
        a = j * per_step + ai
        g = None
        for h in range(P_HEADS):
            sel = rank2_ref[h] < cnt_ref[h, pl.ds(a, 1), :]
            term = jnp.where(sel, e2_ref[h] * e1_ref[h, pl.ds(a, 1), :], 0.0)
            g = term if g is None else g + term
        gates.append(g)
    m = (jnp.concatenate(gates, axis=0) * act).astype(BF16)
    acc_ref[...] += jnp.dot(wu_ref[...], m, preferred_element_type=F32)

    @pl.when(j == pl.num_programs(1) - 1)
    def _():
        o_ref[...] = x_ref[...] + g_ref[0] * acc_ref[...].T


def _peer_dense_call(h, rank2, cnt, e2, e1, w_down, w_up_t, x, gate, tiles_per_batch):
    t, d = h.shape
    e = w_down.shape[0]
    tt = t // (gate.shape[0] * tiles_per_batch)
    ce = 1024
    sel_spec = pl.BlockSpec((P_HEADS, N_KEYS, tt), lambda i, j: (0, 0, i))
    return pl.pallas_call(
        _peer_dense_kernel,
        grid=(t // tt, e // ce),
        in_specs=[pl.BlockSpec((tt, d), lambda i, j: (i, 0)),
                  sel_spec, sel_spec, sel_spec, sel_spec,
                  pl.BlockSpec((ce, d), lambda i, j: (j, 0)),
                  pl.BlockSpec((d, ce), lambda i, j: (0, j)),
                  pl.BlockSpec((tt, d), lambda i, j: (i, 0)),
                  pl.BlockSpec((1, 1, d), lambda i, j: (i // tiles_per_batch, 0, 0))],
        out_specs=pl.BlockSpec((tt, d), lambda i, j: (i, 0)),
        out_shape=jax.ShapeDtypeStruct((t, d), F32),
        scratch_shapes=[pltpu.VMEM((d, tt), F32)],
        compiler_params=_params(("parallel", "arbitrary")),
        name="peer_dense",
    )(h, rank2, cnt, e2, e1, w_down, w_up_t, x, gate)


def _peer_block(x, norm_g, shift, scale, gate, w_pq, sub_keys, w_down, w_up_t):
    b, l, d = x.shape
    h = _normmod_call(x, norm_g, shift, scale).reshape(b * l, d)
    scores = _peer_scores_call(h, w_pq, sub_keys)
    rank2, cnt, e2, e1 = _peer_topk_call(scores)
    tt = 256
    out = _peer_dense_call(h, rank2, cnt, e2, e1, w_down, w_up_t, x.reshape(b * l, d), gate, l // tt)
    return out.reshape(b, l, d)


def kernel(x, c, ctx, c_ctx, w_mod, b_mod, norm1_g, w_in, dw_w, dw_b, cln_g, cln_b, w_conv_out,
           qn_g, kn_g, rpb, w_na_out, w_o, norm2_g, w_pq, sub_keys, w_down, w_up):
    bsz, t, d = x.shape
    n_ctx = ctx.shape[1]
    depth = w_mod.shape[0]
    rows = t // GRID_W
    off_q = 2 * D_CONV
    off_k = off_q + D_ATT
    off_v = off_k + D_ATT
    off_gate = off_v + D_ATT

    n_rows_mod = 8
    cvec = jnp.concatenate([c, c_ctx[None], jnp.zeros((n_rows_mod - bsz - 1, d), F32)], axis=0)
    mod = _mod_call(cvec, w_mod, b_mod[:, None, :])

    cx = ctx
    for l in range(depth):
        last = l == depth - 1
        mod_l = mod[l, :bsz].reshape(bsz, 1, 6, d)
        mod_c = jnp.broadcast_to(mod[l, bsz].reshape(1, 1, 6, d), (bsz, 1, 6, d))
        sh1, sc1, g1, sh2, sc2, g2 = [mod_l[:, :, i] for i in range(6)]
        csh1, csc1, cg1, csh2, csc2, cg2 = [mod_c[:, :, i] for i in range(6)]

        w_in_l = w_in[l]
        w_glu_a = w_in_l[:, :D_CONV].astype(BF16)
        w_glu_g = w_in_l[:, D_CONV:off_q].astype(BF16)
        w_q = w_in_l[:, off_q:off_k].astype(BF16)
        w_k = w_in_l[:, off_k:off_v].astype(BF16)
        w_v = w_in_l[:, off_v:off_gate].astype(BF16)
        w_gate = w_in_l[:, off_gate:].astype(BF16)
        qg_vec = jnp.tile(qn_g[l], NA_HEADS)[None]
        kg_vec = jnp.tile(kn_g[l], NA_HEADS)[None]
        w_conv_l = w_conv_out[l].astype(BF16)
        w_na_l = w_na_out[l].astype(BF16)
        w_o_l = w_o[l].astype(BF16)
        n1g = norm1_g[l][None]
        conv_vecs = (dw_w[l], dw_b[l][None], cln_g[l][None], cln_b[l][None])

        def mixer_inputs(z, shift, scale, need_q):
            b_, l_, _ = z.shape
            h = _normmod_call(z, n1g, shift, scale).reshape(b_ * l_, d)
            k_ = _mm_call(_mm_headnorm_kernel, h, [w_k], [kg_vec], D_ATT, BF16, "proj_k").reshape(b_, l_, D_ATT)
            v_ = _mm_call(_mm_plain_kernel, h, [w_v], [], D_ATT, BF16, "proj_v").reshape(b_, l_, D_ATT)
            if not need_q:
                return h, None, k_, v_
            q_ = _mm_call(_mm_headnorm_kernel, h, [w_q], [qg_vec], D_ATT, BF16, "proj_q").reshape(b_, l_, D_ATT)
            return h, q_, k_, v_

        def mixer_tail(z, h, na, gate):
            b_, l_, _ = z.shape
            glu = _mm_call(_mm_glu_kernel, h, [w_glu_a, w_glu_g], [], D_CONV, F32, "proj_glu")
            sg = _mm_call(_mm_sigmoid_kernel, h, [w_gate], [], 2 * d, F32, "proj_gates")
            cpre = _conv_call(glu.reshape(b_, l_, D_CONV), *conv_vecs).reshape(b_ * l_, D_CONV)
            y = _merge_call(cpre, na.reshape(b_ * l_, D_ATT), sg, w_conv_l, w_na_l)
            return _oproj_call(y.reshape(b_, l_, d), w_o_l, z, gate)

        h_c, q_c, k_c, v_c = mixer_inputs(cx, csh1, csc1, not last)
        h_l, q_l, k_l, v_l = mixer_inputs(x, sh1, sc1, True)
        bias = _na_bias_table(rpb[l], rows)
        na_l = _na_call(q_l, k_l, v_l, k_c, v_c, bias)
        x = mixer_tail(x, h_l, na_l, g1)
        if not last:
            na_c = _ctx_attn_call(q_c, k_c, v_c)
            cx = mixer_tail(cx, h_c, na_c, cg1)

        w_pq_l = w_pq[l].astype(BF16)
        sk_l = sub_keys[l].reshape(2 * P_HEADS, N_KEYS, D_KEY_HALF).astype(BF16)
        w_down_l = w_down[l].astype(BF16)
        w_up_t_l = w_up[l].astype(BF16).T
        n2g = norm2_g[l][None]
        x = _peer_block(x, n2g, sh2, sc2, g2, w_pq_l, sk_l, w_down_l, w_up_t_l)
        if not last:
            cx = _peer_block(cx, n2g, csh2, csc2, cg2, w_pq_l, sk_l, w_down_l, w_up_t_l)
    return x
```

```python
import functools
import math

import numpy as np
import jax
import jax.numpy as jnp
from jax import lax
from jax.experimental import pallas as pl
from jax.experimental.pallas import tpu as pltpu

F32 = jnp.float32
BF16 = jnp.bfloat16

GRID_W = 64
D_CONV = 1024
CONV_W = 31
CONV_PAD = CONV_W // 2
NA_HEADS = 8
NA_DH = 128
D_ATT = NA_HEADS * NA_DH
NA_KH = 8
NA_KW = 16
P_HEADS = 8
N_KEYS = 128
D_KEY_HALF = 128
PK_TOPK = 16
EPS = 1e-6

LANES = 128
SUBLANES = 8
HALO = 16
NA_ROWS_PER_STEP = 4
NA_BAND_BLOCKS = 3
MASK_VALUE = -1e30
VMEM_LIMIT = 56 * 1024 * 1024

NT_DIMS = (((1,), (1,)), ((), ()))


def _params(sem, vmem=VMEM_LIMIT):
    return pltpu.CompilerParams(dimension_semantics=sem, vmem_limit_bytes=vmem)


def _sigmoid(x):
    return 1.0 / (1.0 + jnp.exp(-x))


def _mod_kernel(c_ref, w_ref, b_ref, o_ref):
    c = c_ref[...]
    s = (c * _sigmoid(c)).astype(BF16)
    o_ref[0] = jnp.dot(s, w_ref[0].astype(BF16), preferred_element_type=F32) + b_ref[0]


def _mod_call(cvec, w_mod, b_mod):
    nl, d, n = w_mod.shape
    r = cvec.shape[0]
    tn = 512
    return pl.pallas_call(
        _mod_kernel,
        grid=(nl, n // tn),
        in_specs=[pl.BlockSpec((r, d), lambda l, j: (0, 0)),
                  pl.BlockSpec((1, d, tn), lambda l, j: (l, 0, j)),
                  pl.BlockSpec((1, 1, tn), lambda l, j: (l, 0, j))],
        out_specs=pl.BlockSpec((1, r, tn), lambda l, j: (l, 0, j)),
        out_shape=jax.ShapeDtypeStruct((nl, r, n), F32),
        compiler_params=_params(("parallel", "parallel")),
        name="adaln_mod",
    )(cvec, w_mod, b_mod)


def _normmod_kernel(x_ref, g_ref, sh_ref, sc_ref, o_ref):
    x = x_ref[0]
    ms = jnp.mean(x * x, axis=-1, keepdims=True)
    n = x * lax.rsqrt(ms + EPS) * g_ref[...]
    o_ref[0] = (n * (1.0 + sc_ref[0]) + sh_ref[0]).astype(o_ref.dtype)


def _normmod_call(x, g, shift, scale):
    b, l, d = x.shape
    tl = min(l, 512)
    return pl.pallas_call(
        _normmod_kernel,
        grid=(b, l // tl),
        in_specs=[pl.BlockSpec((1, tl, d), lambda i, j: (i, j, 0)),
                  pl.BlockSpec((1, d), lambda i, j: (0, 0)),
                  pl.BlockSpec((1, 1, d), lambda i, j: (i, 0, 0)),
                  pl.BlockSpec((1, 1, d), lambda i, j: (i, 0, 0))],
        out_specs=pl.BlockSpec((1, tl, d), lambda i, j: (i, j, 0)),
        out_shape=jax.ShapeDtypeStruct((b, l, d), BF16),
        compiler_params=_params(("parallel", "parallel")),
        name="normmod",
    )(x, g, shift, scale)


def _mm_glu_kernel(h_ref, wa_ref, wg_ref, o_ref):
    h = h_ref[...]
    a = jnp.dot(h, wa_ref[...], preferred_element_type=F32)
    g = jnp.dot(h, wg_ref[...], preferred_element_type=F32)
    o_ref[...] = (a * _sigmoid(g)).astype(o_ref.dtype)


def _mm_headnorm_kernel(h_ref, w_ref, g_ref, o_ref):
    y = jnp.dot(h_ref[...], w_ref[...], preferred_element_type=F32)
    for k in range(y.shape[1] // LANES):
        sl = slice(k * LANES, (k + 1) * LANES)
        yk = y[:, sl]
        ms = jnp.mean(yk * yk, axis=-1, keepdims=True)
        o_ref[:, sl] = (yk * lax.rsqrt(ms + EPS) * g_ref[:, sl]).astype(o_ref.dtype)


def _mm_plain_kernel(h_ref, w_ref, o_ref):
    o_ref[...] = jnp.dot(h_ref[...], w_ref[...], preferred_element_type=F32).astype(o_ref.dtype)


def _mm_sigmoid_kernel(h_ref, w_ref, o_ref):
    y = jnp.dot(h_ref[...], w_ref[...], preferred_element_type=F32)
    o_ref[...] = _sigmoid(y).astype(o_ref.dtype)


def _mm_call(kernel, h, ws, extras, n_out, out_dtype, name):
    t, k = h.shape
    tm = min(t, 1024)
    tn = 512
    in_specs = [pl.BlockSpec((tm, k), lambda i, j: (i, 0))]
    in_specs += [pl.BlockSpec((k, tn), lambda i, j: (0, j)) for _ in ws]
    in_specs += [pl.BlockSpec((1, tn), lambda i, j: (0, j)) for _ in extras]
    return pl.pallas_call(
        kernel,
        grid=(t // tm, n_out // tn),
        in_specs=in_specs,
        out_specs=pl.BlockSpec((tm, tn), lambda i, j: (i, j)),
        out_shape=jax.ShapeDtypeStruct((t, n_out), out_dtype),
        compiler_params=_params(("parallel", "parallel")),
        name=name,
    )(h, *ws, *extras)


def _softmax_pv(s_list, v_list):
    m = s_list[0].max(axis=-1, keepdims=True)
    for s in s_list[1:]:
        m = jnp.maximum(m, s.max(axis=-1, keepdims=True))
    num = None
    den = None
    for s, v in zip(s_list, v_list):
        p = jnp.exp(s - m)
        d = p.sum(axis=-1, keepdims=True)
        o = jnp.dot(p.astype(BF16), v, preferred_element_type=F32)
        num = o if num is None else num + o
        den = d if den is None else den + d
    return num / den


def _na_kernel(q_ref, k0_ref, k1_ref, k2_ref, v0_ref, v1_ref, v2_ref, kc_ref, vc_ref, bias_ref, o_ref):
    scale = NA_DH ** -0.5
    for h in range(NA_HEADS):
        sl = slice(h * NA_DH, (h + 1) * NA_DH)
        q = q_ref[0, :, sl]
        kb = jnp.concatenate([k0_ref[0, :, sl], k1_ref[0, :, sl], k2_ref[0, :, sl]], axis=0)
        vb = jnp.concatenate([v0_ref[0, :, sl], v1_ref[0, :, sl], v2_ref[0, :, sl]], axis=0)
        s_w = lax.dot_general(q, kb, NT_DIMS, preferred_element_type=F32) * scale + bias_ref[0, h]
        s_c = lax.dot_general(q, kc_ref[0, :, sl], NT_DIMS, preferred_element_type=F32) * scale
        o = _softmax_pv([s_w, s_c], [vb, vc_ref[0, :, sl]])
        o_ref[0, :, sl] = o.astype(o_ref.dtype)


def _na_bias_table(rpb_l):
    r4 = NA_ROWS_PER_STEP
    band_rows = NA_BAND_BLOCKS * r4
    n_dr, n_dc = 2 * NA_KH - 1, 2 * NA_KW - 1
    qc = np.arange(GRID_W)[:, None]
    kc = np.arange(GRID_W)[None, :]
    cs = np.clip(qc - NA_KW // 2, 0, GRID_W - NA_KW)
    col_ok = (kc >= cs) & (kc < cs + NA_KW)
    col_sel = (np.arange(n_dc)[:, None, None] == (kc - qc + NA_KW - 1)[None]) & col_ok[None]
    qr = np.arange(r4)[:, None]
    kr = np.arange(band_rows)[None, :]
    row_sel, row_ok = [], []
    for off, ks in ((0, np.zeros_like(qr)), (r4, qr), (band_rows - r4, np.full_like(qr, band_rows - NA_KH))):
        ok = (kr >= ks) & (kr < ks + NA_KH)
        row_ok.append(ok)
        row_sel.append((np.arange(n_dr)[None, None, :] == (kr - off - qr + NA_KH - 1)[:, :, None]) & ok[:, :, None])
    row_sel = jnp.asarray(np.stack(row_sel), F32)
    col_sel = jnp.asarray(col_sel, F32)
    bias = jnp.einsum("vqkr,hrc,cxy->vhqxky", row_sel, rpb_l, col_sel, precision=lax.Precision.HIGHEST)
    ok = np.stack(row_ok)[:, None, :, None, :, None] & col_ok[None, None, None, :, None, :]
    bias = jnp.where(ok, bias, MASK_VALUE)
    return bias.reshape(3, NA_HEADS, r4 * GRID_W, band_rows * GRID_W)


def _na_call(q, k, v, kc, vc, bias):
    b, t, _ = q.shape
    c = kc.shape[1]
    tq = NA_ROWS_PER_STEP * GRID_W
    nb = t // tq
    assert nb >= NA_BAND_BLOCKS + 1 and NA_KH == 2 * NA_ROWS_PER_STEP

    def band(d):
        return pl.BlockSpec((1, tq, D_ATT), lambda i, j: (i, jnp.clip(j - 1, 0, nb - NA_BAND_BLOCKS) + d, 0))

    def variant(i, j):
        return (jnp.where(j == 0, 0, jnp.where(j == nb - 1, 2, 1)), 0, 0, 0)

    return pl.pallas_call(
        _na_kernel,
        grid=(b, nb),
        in_specs=[pl.BlockSpec((1, tq, D_ATT), lambda i, j: (i, j, 0)),
                  band(0), band(1), band(2), band(0), band(1), band(2),
                  pl.BlockSpec((1, c, D_ATT), lambda i, j: (i, 0, 0)),
                  pl.BlockSpec((1, c, D_ATT), lambda i, j: (i, 0, 0)),
                  pl.BlockSpec((1, NA_HEADS, tq, NA_BAND_BLOCKS * tq), variant)],
        out_specs=pl.BlockSpec((1, tq, D_ATT), lambda i, j: (i, j, 0)),
        out_shape=jax.ShapeDtypeStruct((b, t, D_ATT), BF16),
        compiler_params=_params(("parallel", "arbitrary")),
        name="neighborhood_attention",
    )(q, k, k, k, v, v, v, kc, vc, bias)


def _ctx_attn_kernel(q_ref, k_ref, v_ref, o_ref):
    scale = NA_DH ** -0.5
    for h in range(NA_HEADS):
        sl = slice(h * NA_DH, (h + 1) * NA_DH)
        s = lax.dot_general(q_ref[0, :, sl], k_ref[0, :, sl], NT_DIMS, preferred_element_type=F32) * scale
        o_ref[0, :, sl] = _softmax_pv([s], [v_ref[0, :, sl]]).astype(o_ref.dtype)


def _ctx_attn_call(q, k, v):
    b, c, _ = q.shape
    spec = pl.BlockSpec((1, c, D_ATT), lambda i: (i, 0, 0))
    return pl.pallas_call(
        _ctx_attn_kernel,
        grid=(b,),
        in_specs=[spec, spec, spec],
        out_specs=spec,
        out_shape=jax.ShapeDtypeStruct((b, c, D_ATT), BF16),
        compiler_params=_params(("parallel",)),
        name="context_attention",
    )(q, k, v)


def _conv_kernel(prev_ref, cur_ref, next_ref, w_ref, b_ref, g_ref, bb_ref, o_ref, buf_ref):
    i = pl.program_id(1)
    tl = cur_ref.shape[1]
    buf_ref[0, 0:HALO] = jnp.where(i > 0, prev_ref[0], 0.0)
    buf_ref[0, HALO:HALO + tl] = cur_ref[0]
    buf_ref[0, HALO + tl:2 * HALO + tl] = jnp.where(i < pl.num_programs(1) - 1, next_ref[0], 0.0)
    span = tl + 2 * HALO - SUBLANES
    for r in range(1, SUBLANES):
        buf_ref[r, 0:span] = buf_ref[0, r:r + span]
    rows = 32
    base = HALO - CONV_PAD
    for r0 in range(0, tl, rows):
        acc = jnp.zeros((rows, D_CONV), F32) + b_ref[...]
        for k in range(CONV_W):
            q, r = divmod(base + k, SUBLANES)
            start = r0 + q * SUBLANES
            acc = acc + buf_ref[r, start:start + rows, :] * w_ref[k:k + 1, :]
        mu = jnp.mean(acc, axis=-1, keepdims=True)
        cen = acc - mu
        var = jnp.mean(cen * cen, axis=-1, keepdims=True)
        y = cen * lax.rsqrt(var + EPS) * g_ref[...] + bb_ref[...]
        o_ref[0, r0:r0 + rows, :] = (y * _sigmoid(y)).astype(o_ref.dtype)


def _conv_call(glu, dw_w, dw_b, ln_g, ln_b):
    b, l, c = glu.shape
    tl = min(l, 128)
    hb = tl // HALO
    last_halo = l // HALO - 1
    vec = pl.BlockSpec((1, c), lambda i, j: (0, 0))
    return pl.pallas_call(
        _conv_kernel,
        grid=(b, l // tl),
        in_specs=[pl.BlockSpec((1, HALO, c), lambda i, j: (i, jnp.maximum(j * hb - 1, 0), 0)),
                  pl.BlockSpec((1, tl, c), lambda i, j: (i, j, 0)),
                  pl.BlockSpec((1, HALO, c), lambda i, j: (i, jnp.minimum((j + 1) * hb, last_halo), 0)),
                  pl.BlockSpec((CONV_W, c), lambda i, j: (0, 0)),
                  vec, vec, vec],
        out_specs=pl.BlockSpec((1, tl, c), lambda i, j: (i, j, 0)),
        out_shape=jax.ShapeDtypeStruct((b, l, c), BF16),
        scratch_shapes=[pltpu.VMEM((SUBLANES, tl + 2 * HALO, c), F32)],
        compiler_params=_params(("parallel", "arbitrary")),
        name="conformer_conv",
    )(glu, glu, glu, dw_w, dw_b, ln_g, ln_b)


def _merge_kernel(cp_ref, na_ref, sgc_ref, sgn_ref, wc_ref, wn_ref, y_ref):
    conv = jnp.dot(cp_ref[...], wc_ref[...], preferred_element_type=F32)
    na = jnp.dot(na_ref[...], wn_ref[...], preferred_element_type=F32)
    y_ref[...] = (sgc_ref[...] * conv + sgn_ref[...] * na).astype(y_ref.dtype)


def _merge_call(cpre, na, sg, w_conv_out, w_na_out):
    t, kc = cpre.shape
    d = w_conv_out.shape[1]
    tm = min(t, 512)
    return pl.pallas_call(
        _merge_kernel,
        grid=(t // tm,),
        in_specs=[pl.BlockSpec((tm, kc), lambda i: (i, 0)),
                  pl.BlockSpec((tm, kc), lambda i: (i, 0)),
                  pl.BlockSpec((tm, d), lambda i: (i, 0)),
                  pl.BlockSpec((tm, d), lambda i: (i, 1)),
                  pl.BlockSpec((kc, d), lambda i: (0, 0)),
                  pl.BlockSpec((kc, d), lambda i: (0, 0))],
        out_specs=pl.BlockSpec((tm, d), lambda i: (i, 0)),
        out_shape=jax.ShapeDtypeStruct((t, d), BF16),
        compiler_params=_params(("parallel",)),
        name="branch_merge",
    )(cpre, na, sg, sg, w_conv_out, w_na_out)


def _oproj_kernel(y_ref, w_ref, x_ref, g_ref, o_ref):
    o_ref[0] = x_ref[0] + g_ref[0] * jnp.dot(y_ref[0], w_ref[...], preferred_element_type=F32)


def _oproj_call(y, w_o, x, gate):
    b, l, d = x.shape
    tm = min(l, 512)
    return pl.pallas_call(
        _oproj_kernel,
        grid=(b, l // tm),
        in_specs=[pl.BlockSpec((1, tm, d), lambda i, j: (i, j, 0)),
                  pl.BlockSpec((d, d), lambda i, j: (0, 0)),
                  pl.BlockSpec((1, tm, d), lambda i, j: (i, j, 0)),
                  pl.BlockSpec((1, 1, d), lambda i, j: (i, 0, 0))],
        out_specs=pl.BlockSpec((1, tm, d), lambda i, j: (i, j, 0)),
        out_shape=jax.ShapeDtypeStruct((b, l, d), F32),
        compiler_params=_params(("parallel", "parallel")),
        name="out_proj_residual",
    )(y, w_o, x, gate)


def _peer_scores_kernel(h_ref, w_ref, sk_ref, o_ref):
    q = jnp.dot(h_ref[...], w_ref[...], preferred_element_type=F32)
    for g in range(2 * P_HEADS):
        qg = q[:, g * D_KEY_HALF:(g + 1) * D_KEY_HALF]
        ms = jnp.mean(qg * qg, axis=-1, keepdims=True)
        qn = (qg * lax.rsqrt(ms + EPS)).astype(BF16)
        o_ref[g] = lax.dot_general(sk_ref[g], qn, NT_DIMS, preferred_element_type=F32)


def _peer_scores_call(h, w_pq, sub_keys):
    t, d = h.shape
    ng = 2 * P_HEADS
    tm = min(t, 256)
    return pl.pallas_call(
        _peer_scores_kernel,
        grid=(t // tm,),
        in_specs=[pl.BlockSpec((tm, d), lambda i: (i, 0)),
                  pl.BlockSpec((d, ng * D_KEY_HALF), lambda i: (0, 0)),
                  pl.BlockSpec((ng, N_KEYS, D_KEY_HALF), lambda i: (0, 0, 0))],
        out_specs=pl.BlockSpec((ng, N_KEYS, tm), lambda i: (0, 0, i)),
        out_shape=jax.ShapeDtypeStruct((ng, N_KEYS, t), F32),
        compiler_params=_params(("parallel",)),
        name="peer_scores",
    )(h, w_pq, sub_keys)


def _top16_rows(s, iota, tie_break, want_rank):
    rank = jnp.full(s.shape, float(PK_TOPK), F32) if want_rank else None
    vals = []
    for k in range(PK_TOPK):
        m = jnp.max(s, axis=0, keepdims=True)
        if tie_break:
            idx = jnp.min(jnp.where(s == m, iota, float(N_KEYS)), axis=0, keepdims=True)
            sel = iota == idx
        else:
            sel = s == m
        if want_rank:
            rank = jnp.where(sel, float(k), rank)
        s = jnp.where(sel, -jnp.inf, s)
        vals.append(m)
    taken = jnp.sum(jnp.where(s == -jnp.inf, 1.0, 0.0), axis=0, keepdims=True)
    return vals, rank, taken


def _peer_topk_kernel(s_ref, rank2_ref, cnt_ref, e2_ref, e1_ref):
    width = LANES
    n_chunks = s_ref.shape[2] // width
    n_cand = 80
    r = lax.broadcasted_iota(jnp.int32, (n_cand, width), 0)
    ci = jnp.where(r < 16, 0, jnp.where(r < 72, 1 + ((r - 16) >> 3), r - 64))
    cj = jnp.where(r < 16, r, jnp.where(r < 72, (r - 16) & 7, 0))
    valid = (ci + 1) * (cj + 1) <= PK_TOPK
    pos = (ci * PK_TOPK + cj).astype(F32)
    iota = lax.broadcasted_iota(jnp.int32, (N_KEYS, width), 0).astype(F32)

    def select(ls, tie_break):
        s1 = s_ref[0, :, ls]
        s2 = s_ref[1, :, ls]
        v1, rank1, taken1 = _top16_rows(s1, iota, tie_break, tie_break)
        v2, rank2, taken2 = _top16_rows(s2, iota, tie_break, True)
        v2_all = jnp.concatenate(v2, axis=0)
        blocks = [v1[0] + v2_all]
        for i in range(1, 8):
            blocks.append(v1[i] + v2_all[0:8])
        blocks.append(jnp.concatenate(v1[8:], axis=0) + v2[0])
        cand = jnp.where(valid, jnp.concatenate(blocks, axis=0), -jnp.inf)
        selm = jnp.zeros((n_cand, width), F32)
        top = None
        z = None
        for k in range(PK_TOPK):
            m = jnp.max(cand, axis=0, keepdims=True)
            if tie_break:
                pmin = jnp.min(jnp.where(cand == m, pos, 1e9), axis=0, keepdims=True)
                sel = pos == pmin
            else:
                sel = cand == m
            selm = jnp.where(sel, 1.0, selm)
            cand = jnp.where(sel, -jnp.inf, cand)
            if k == 0:
                top = m
                z = jnp.ones_like(m)
            else:
                z = z + jnp.exp(m - top)
        counts = [jnp.sum(selm[0:16], axis=0, keepdims=True)]
        for i in range(1, 8):
            counts.append(jnp.sum(selm[8 + 8 * i:16 + 8 * i], axis=0, keepdims=True))
        for q in range(8):
            counts.append(selm[72 + q:73 + q])
        cnt = jnp.zeros((N_KEYS, width), F32)
        for i in range(PK_TOPK):
            hit = (rank1 == float(i)) if tie_break else (s1 == v1[i])
            cnt = jnp.where(hit, counts[i], cnt)
        rank2_ref[0, :, ls] = rank2.astype(rank2_ref.dtype)
        cnt_ref[0, :, ls] = cnt
        e2_ref[0, :, ls] = jnp.exp(s2 - v2[0]).astype(e2_ref.dtype)
        e1_ref[0, :, ls] = jnp.exp(s1 - v1[0]) / z
        taken = counts[0]
        for cnt_i in counts[1:]:
            taken = taken + cnt_i
        return taken1 + taken2 + taken

    def chunk(c, carry):
        ls = pl.ds(pl.multiple_of(c * width, width), width)
        taken = select(ls, tie_break=False)

        @pl.when(jnp.max(taken) > 3.0 * PK_TOPK)
        def _():
            select(ls, tie_break=True)

        return carry

    lax.fori_loop(0, n_chunks, chunk, 0)


def _peer_topk_call(scores):
    ng, nk, t = scores.shape
    tt = min(t, 512)
    out_f = jax.ShapeDtypeStruct((P_HEADS, nk, t), F32)
    out_b = jax.ShapeDtypeStruct((P_HEADS, nk, t), BF16)
    ospec = pl.BlockSpec((1, nk, tt), lambda i, h: (h, 0, i))
    return pl.pallas_call(
        _peer_topk_kernel,
        grid=(t // tt, P_HEADS),
        in_specs=[pl.BlockSpec((2, nk, tt), lambda i, h: (h, 0, i))],
        out_specs=[ospec, ospec, ospec, ospec],
        out_shape=[out_b, out_f, out_b, out_f],
        compiler_params=_params(("parallel", "parallel")),
        name="peer_topk",
    )(scores)


BF16_ROWS = 16


def _peer_dense_kernel(h_ref, rank2_ref, cnt_ref, e2_ref, e1_ref, wd_ref, wu_ref, x_ref, g_ref, o_ref,
                       acc_ref, m_ref):
    j = pl.program_id(1)
    ce, tt = m_ref.shape
    groups = N_KEYS // BF16_ROWS

    @pl.when(j == 0)
    def _():
        acc_ref[...] = jnp.zeros_like(acc_ref)

    a_t = lax.dot_general(wd_ref[...], h_ref[...], NT_DIMS, preferred_element_type=F32)
    for ai in range(ce // N_KEYS):
        rows = slice(ai * N_KEYS, (ai + 1) * N_KEYS)
        g = None
        for h in range(P_HEADS):
            cn = jnp.broadcast_to(cnt_ref[h, ai:ai + 1, :], (BF16_ROWS, tt)).astype(BF16)[None]
            e1 = jnp.broadcast_to(e1_ref[h, ai:ai + 1, :], (BF16_ROWS, tt)).astype(BF16)[None]
            r2 = rank2_ref[h].reshape(groups, BF16_ROWS, tt)
            e2 = e2_ref[h].reshape(groups, BF16_ROWS, tt)
            term = jnp.where(r2 < cn, e2 * e1, jnp.zeros_like(e2))
            g = term if g is None else g + term
        at = a_t[rows, :]
        act = (at * (lax.erf(at * (1.0 / math.sqrt(2.0))) + 1.0) * 0.5).astype(BF16)
        m_ref[rows, :] = g.reshape(N_KEYS, tt) * act
    acc_ref[...] += jnp.dot(wu_ref[...], m_ref[...], preferred_element_type=F32)

    @pl.when(j == pl.num_programs(1) - 1)
    def _():
        o_ref[...] = x_ref[...] + g_ref[0] * acc_ref[...].T


def _peer_dense_call(h, rank2, cnt, e2, e1, w_down, w_up_t, x, gate):
    t, d = h.shape
    e = w_down.shape[0]
    tt = 512 if t % 512 == 0 else 256
    ce = 1024
    per_step = ce // N_KEYS
    tiles_per_gate = t // (gate.shape[0] * tt)
    cnt = cnt.reshape(P_HEADS, e // ce, per_step, t)
    e1 = e1.reshape(P_HEADS, e // ce, per_step, t)
    once = dict(pipeline_mode=pl.Buffered(1))
    sel_spec = pl.BlockSpec((P_HEADS, N_KEYS, tt), lambda i, j: (0, 0, i), **once)
    row_spec = pl.BlockSpec((P_HEADS, None, per_step, tt), lambda i, j: (0, j, 0, i))
    return pl.pallas_call(
        _peer_dense_kernel,
        grid=(t // tt, e // ce),
        in_specs=[pl.BlockSpec((tt, d), lambda i, j: (i, 0), **once),
                  sel_spec, row_spec, sel_spec, row_spec,
                  pl.BlockSpec((ce, d), lambda i, j: (j, 0)),
                  pl.BlockSpec((d, ce), lambda i, j: (0, j)),
                  pl.BlockSpec((tt, d), lambda i, j: (i, 0), **once),
                  pl.BlockSpec((1, 1, d), lambda i, j: (i // tiles_per_gate, 0, 0))],
        out_specs=pl.BlockSpec((tt, d), lambda i, j: (i, 0)),
        out_shape=jax.ShapeDtypeStruct((t, d), F32),
        scratch_shapes=[pltpu.VMEM((d, tt), F32), pltpu.VMEM((ce, tt), BF16)],
        compiler_params=_params(("parallel", "arbitrary"), 60 * 1024 * 1024),
        name="peer_dense",
    )(h, rank2, cnt, e2, e1, w_down, w_up_t, x, gate)


def _peer_block(x, norm_g, shift, scale, gate, w_pq, sub_keys, w_down, w_up_t):
    b, l, d = x.shape
    h = _normmod_call(x, norm_g, shift, scale).reshape(b * l, d)
    scores = _peer_scores_call(h, w_pq, sub_keys)
    rank2, cnt, e2, e1 = _peer_topk_call(scores)
    out = _peer_dense_call(h, rank2, cnt, e2, e1, w_down, w_up_t, x.reshape(b * l, d), gate)
    return out.reshape(b, l, d)


def kernel(x, c, ctx, c_ctx, w_mod, b_mod, norm1_g, w_in, dw_w, dw_b, cln_g, cln_b, w_conv_out,
           qn_g, kn_g, rpb, w_na_out, w_o, norm2_g, w_pq, sub_keys, w_down, w_up):
    bsz, t, d = x.shape
    n_ctx = ctx.shape[1]
    depth = w_mod.shape[0]
    off_q = 2 * D_CONV
    off_k = off_q + D_ATT
    off_v = off_k + D_ATT
    off_gate = off_v + D_ATT

    n_rows_mod = 8
    cvec = jnp.concatenate([c, c_ctx[None], jnp.zeros((n_rows_mod - bsz - 1, d), F32)], axis=0)
    mod = _mod_call(cvec, w_mod, b_mod[:, None, :])

    cx = ctx
    for l in range(depth):
        last = l == depth - 1
        mod_l = mod[l, :bsz].reshape(bsz, 1, 6, d)
        mod_c = jnp.broadcast_to(mod[l, bsz].reshape(1, 1, 6, d), (bsz, 1, 6, d))
        sh1, sc1, g1, sh2, sc2, g2 = [mod_l[:, :, i] for i in range(6)]
        csh1, csc1, cg1, csh2, csc2, cg2 = [mod_c[:, :, i] for i in range(6)]

        w_in_l = w_in[l]
        w_glu_a = w_in_l[:, :D_CONV].astype(BF16)
        w_glu_g = w_in_l[:, D_CONV:off_q].astype(BF16)
        w_q = w_in_l[:, off_q:off_k].astype(BF16)
        w_k = w_in_l[:, off_k:off_v].astype(BF16)
        w_v = w_in_l[:, off_v:off_gate].astype(BF16)
        w_gate = w_in_l[:, off_gate:].astype(BF16)
        qg_vec = jnp.tile(qn_g[l], NA_HEADS)[None]
        kg_vec = jnp.tile(kn_g[l], NA_HEADS)[None]
        w_conv_l = w_conv_out[l].astype(BF16)
        w_na_l = w_na_out[l].astype(BF16)
        w_o_l = w_o[l].astype(BF16)
        n1g = norm1_g[l][None]
        conv_vecs = (dw_w[l], dw_b[l][None], cln_g[l][None], cln_b[l][None])

        def mixer_inputs(z, shift, scale, need_q):
            b_, l_, _ = z.shape
            h = _normmod_call(z, n1g, shift, scale).reshape(b_ * l_, d)
            k_ = _mm_call(_mm_headnorm_kernel, h, [w_k], [kg_vec], D_ATT, BF16, "proj_k").reshape(b_, l_, D_ATT)
            v_ = _mm_call(_mm_plain_kernel, h, [w_v], [], D_ATT, BF16, "proj_v").reshape(b_, l_, D_ATT)
            if not need_q:
                return h, None, k_, v_
            q_ = _mm_call(_mm_headnorm_kernel, h, [w_q], [qg_vec], D_ATT, BF16, "proj_q").reshape(b_, l_, D_ATT)
            return h, q_, k_, v_

        def mixer_tail(z, h, na, gate):
            b_, l_, _ = z.shape
            glu = _mm_call(_mm_glu_kernel, h, [w_glu_a, w_glu_g], [], D_CONV, F32, "proj_glu")
            sg = _mm_call(_mm_sigmoid_kernel, h, [w_gate], [], 2 * d, F32, "proj_gates")
            cpre = _conv_call(glu.reshape(b_, l_, D_CONV), *conv_vecs).reshape(b_ * l_, D_CONV)
            y = _merge_call(cpre, na.reshape(b_ * l_, D_ATT), sg, w_conv_l, w_na_l)
            return _oproj_call(y.reshape(b_, l_, d), w_o_l, z, gate)

        h_c, q_c, k_c, v_c = mixer_inputs(cx, csh1, csc1, not last)
        h_l, q_l, k_l, v_l = mixer_inputs(x, sh1, sc1, True)
        bias = _na_bias_table(rpb[l])
        na_l = _na_call(q_l, k_l, v_l, k_c, v_c, bias)
        x = mixer_tail(x, h_l, na_l, g1)
        if not last:
            na_c = _ctx_attn_call(q_c, k_c, v_c)
            cx = mixer_tail(cx, h_c, na_c, cg1)

        w_pq_l = w_pq[l].astype(BF16)
        sk_l = sub_keys[l].reshape(2 * P_HEADS, N_KEYS, D_KEY_HALF).astype(BF16)
        w_down_l = w_down[l].astype(BF16)
        w_up_t_l = w_up[l].astype(BF16).T
        n2g = norm2_g[l][None]
        x = _peer_block(x, n2g, sh2, sc2, g2, w_pq_l, sk_l, w_down_l, w_up_t_l)
        if not last:
            cx = _peer_block(cx, n2g, csh2, csc2, cg2[:1], w_pq_l, sk_l, w_down_l, w_up_t_l)
    return x
```

```python
import functools
import math

import numpy as np
import jax
import jax.numpy as jnp
from jax import lax
from jax.experimental import pallas as pl
from jax.experimental.pallas import tpu as pltpu

F32 = jnp.float32
BF16 = jnp.bfloat16

GRID_W = 64
D_CONV = 1024
CONV_W = 31
CONV_PAD = CONV_W // 2
NA_HEADS = 8
NA_DH = 128
D_ATT = NA_HEADS * NA_DH
NA_KH = 8
NA_KW = 16
P_HEADS = 8
N_KEYS = 128
D_KEY_HALF = 128
PK_TOPK = 16
EPS = 1e-6

LANES = 128
SUBLANES = 8
HALO = 16
NA_ROWS_PER_STEP = 4
NA_BAND_BLOCKS = 3
MASK_VALUE = -1e30
VMEM_LIMIT = 56 * 1024 * 1024

NT_DIMS = (((1,), (1,)), ((), ()))


def _params(sem, vmem=VMEM_LIMIT):
    return pltpu.CompilerParams(dimension_semantics=sem, vmem_limit_bytes=vmem)


def _sigmoid(x):
    return 1.0 / (1.0 + jnp.exp(-x))


def _mod_kernel(c_ref, w_ref, b_ref, o_ref):
    c = c_ref[...]
    s = (c * _sigmoid(c)).astype(BF16)
    o_ref[0] = jnp.dot(s, w_ref[0].astype(BF16), preferred_element_type=F32) + b_ref[0]


def _mod_call(cvec, w_mod, b_mod):
    nl, d, n = w_mod.shape
    r = cvec.shape[0]
    tn = 512
    return pl.pallas_call(
        _mod_kernel,
        grid=(nl, n // tn),
        in_specs=[pl.BlockSpec((r, d), lambda l, j: (0, 0)),
                  pl.BlockSpec((1, d, tn), lambda l, j: (l, 0, j)),
                  pl.BlockSpec((1, 1, tn), lambda l, j: (l, 0, j))],
        out_specs=pl.BlockSpec((1, r, tn), lambda l, j: (l, 0, j)),
        out_shape=jax.ShapeDtypeStruct((nl, r, n), F32),
        compiler_params=_params(("parallel", "parallel")),
        name="adaln_mod",
    )(cvec, w_mod, b_mod)


def _normmod_kernel(x_ref, g_ref, sh_ref, sc_ref, o_ref):
    x = x_ref[0]
    ms = jnp.mean(x * x, axis=-1, keepdims=True)
    n = x * lax.rsqrt(ms + EPS) * g_ref[...]
    o_ref[0] = (n * (1.0 + sc_ref[0]) + sh_ref[0]).astype(o_ref.dtype)


def _normmod_call(x, g, shift, scale):
    b, l, d = x.shape
    tl = min(l, 512)
    return pl.pallas_call(
        _normmod_kernel,
        grid=(b, l // tl),
        in_specs=[pl.BlockSpec((1, tl, d), lambda i, j: (i, j, 0)),
                  pl.BlockSpec((1, d), lambda i, j: (0, 0)),
                  pl.BlockSpec((1, 1, d), lambda i, j: (i, 0, 0)),
                  pl.BlockSpec((1, 1, d), lambda i, j: (i, 0, 0))],
        out_specs=pl.BlockSpec((1, tl, d), lambda i, j: (i, j, 0)),
        out_shape=jax.ShapeDtypeStruct((b, l, d), BF16),
        compiler_params=_params(("parallel", "parallel")),
        name="normmod",
    )(x, g, shift, scale)


def _mm_glu_kernel(h_ref, wa_ref, wg_ref, o_ref):
    h = h_ref[...]
    a = jnp.dot(h, wa_ref[...], preferred_element_type=F32)
    g = jnp.dot(h, wg_ref[...], preferred_element_type=F32)
    o_ref[...] = (a * _sigmoid(g)).astype(o_ref.dtype)


def _mm_headnorm_kernel(h_ref, w_ref, g_ref, o_ref):
    y = jnp.dot(h_ref[...], w_ref[...], preferred_element_type=F32)
    for k in range(y.shape[1] // LANES):
        sl = slice(k * LANES, (k + 1) * LANES)
        yk = y[:, sl]
        ms = jnp.mean(yk * yk, axis=-1, keepdims=True)
        o_ref[:, sl] = (yk * lax.rsqrt(ms + EPS) * g_ref[:, sl]).astype(o_ref.dtype)


def _mm_plain_kernel(h_ref, w_ref, o_ref):
    o_ref[...] = jnp.dot(h_ref[...], w_ref[...], preferred_element_type=F32).astype(o_ref.dtype)


def _mm_sigmoid_kernel(h_ref, w_ref, o_ref):
    y = jnp.dot(h_ref[...], w_ref[...], preferred_element_type=F32)
    o_ref[...] = _sigmoid(y).astype(o_ref.dtype)


def _mm_call(kernel, h, ws, extras, n_out, out_dtype, name):
    t, k = h.shape
    tm = min(t, 1024)
    tn = 512
    in_specs = [pl.BlockSpec((tm, k), lambda i, j: (i, 0))]
    in_specs += [pl.BlockSpec((k, tn), lambda i, j: (0, j)) for _ in ws]
    in_specs += [pl.BlockSpec((1, tn), lambda i, j: (0, j)) for _ in extras]
    return pl.pallas_call(
        kernel,
        grid=(t // tm, n_out // tn),
        in_specs=in_specs,
        out_specs=pl.BlockSpec((tm, tn), lambda i, j: (i, j)),
        out_shape=jax.ShapeDtypeStruct((t, n_out), out_dtype),
        compiler_params=_params(("parallel", "parallel")),
        name=name,
    )(h, *ws, *extras)


def _softmax_pv(s_list, v_list):
    m = s_list[0].max(axis=-1, keepdims=True)
    for s in s_list[1:]:
        m = jnp.maximum(m, s.max(axis=-1, keepdims=True))
    num = None
    den = None
    for s, v in zip(s_list, v_list):
        p = jnp.exp(s - m)
        d = p.sum(axis=-1, keepdims=True)
        o = jnp.dot(p.astype(BF16), v, preferred_element_type=F32)
        num = o if num is None else num + o
        den = d if den is None else den + d
    return num / den


def _na_kernel(q_ref, k0_ref, k1_ref, k2_ref, v0_ref, v1_ref, v2_ref, kc_ref, vc_ref, bias_ref, o_ref):
    scale = NA_DH ** -0.5
    for h in range(NA_HEADS):
        sl = slice(h * NA_DH, (h + 1) * NA_DH)
        q = q_ref[0, :, sl]
        kb = jnp.concatenate([k0_ref[0, :, sl], k1_ref[0, :, sl], k2_ref[0, :, sl]], axis=0)
        vb = jnp.concatenate([v0_ref[0, :, sl], v1_ref[0, :, sl], v2_ref[0, :, sl]], axis=0)
        s_w = lax.dot_general(q, kb, NT_DIMS, preferred_element_type=F32) * scale + bias_ref[0, h]
        s_c = lax.dot_general(q, kc_ref[0, :, sl], NT_DIMS, preferred_element_type=F32) * scale
        o = _softmax_pv([s_w, s_c], [vb, vc_ref[0, :, sl]])
        o_ref[0, :, sl] = o.astype(o_ref.dtype)


def _na_bias_table(rpb_l):
    r4 = NA_ROWS_PER_STEP
    band_rows = NA_BAND_BLOCKS * r4
    n_dr, n_dc = 2 * NA_KH - 1, 2 * NA_KW - 1
    qc = np.arange(GRID_W)[:, None]
    kc = np.arange(GRID_W)[None, :]
    cs = np.clip(qc - NA_KW // 2, 0, GRID_W - NA_KW)
    col_ok = (kc >= cs) & (kc < cs + NA_KW)
    col_sel = (np.arange(n_dc)[:, None, None] == (kc - qc + NA_KW - 1)[None]) & col_ok[None]
    qr = np.arange(r4)[:, None]
    kr = np.arange(band_rows)[None, :]
    row_sel, row_ok = [], []
    for off, ks in ((0, np.zeros_like(qr)), (r4, qr), (band_rows - r4, np.full_like(qr, band_rows - NA_KH))):
        ok = (kr >= ks) & (kr < ks + NA_KH)
        row_ok.append(ok)
        row_sel.append((np.arange(n_dr)[None, None, :] == (kr - off - qr + NA_KH - 1)[:, :, None]) & ok[:, :, None])
    row_sel = jnp.asarray(np.stack(row_sel), F32)
    col_sel = jnp.asarray(col_sel, F32)
    bias = jnp.einsum("vqkr,hrc,cxy->vhqxky", row_sel, rpb_l, col_sel, precision=lax.Precision.HIGHEST)
    ok = np.stack(row_ok)[:, None, :, None, :, None] & col_ok[None, None, None, :, None, :]
    bias = jnp.where(ok, bias, MASK_VALUE)
    return bias.reshape(3, NA_HEADS, r4 * GRID_W, band_rows * GRID_W)


def _na_call(q, k, v, kc, vc, bias):
    b, t, _ = q.shape
    c = kc.shape[1]
    tq = NA_ROWS_PER_STEP * GRID_W
    nb = t // tq
    assert nb >= NA_BAND_BLOCKS + 1 and NA_KH == 2 * NA_ROWS_PER_STEP

    def band(d):
        return pl.BlockSpec((1, tq, D_ATT), lambda i, j: (i, jnp.clip(j - 1, 0, nb - NA_BAND_BLOCKS) + d, 0))

    def variant(i, j):
        return (jnp.where(j == 0, 0, jnp.where(j == nb - 1, 2, 1)), 0, 0, 0)

    return pl.pallas_call(
        _na_kernel,
        grid=(b, nb),
        in_specs=[pl.BlockSpec((1, tq, D_ATT), lambda i, j: (i, j, 0)),
                  band(0), band(1), band(2), band(0), band(1), band(2),
                  pl.BlockSpec((1, c, D_ATT), lambda i, j: (i, 0, 0)),
                  pl.BlockSpec((1, c, D_ATT), lambda i, j: (i, 0, 0)),
                  pl.BlockSpec((1, NA_HEADS, tq, NA_BAND_BLOCKS * tq), variant)],
        out_specs=pl.BlockSpec((1, tq, D_ATT), lambda i, j: (i, j, 0)),
        out_shape=jax.ShapeDtypeStruct((b, t, D_ATT), BF16),
        compiler_params=_params(("parallel", "arbitrary")),
        name="neighborhood_attention",
    )(q, k, k, k, v, v, v, kc, vc, bias)


def _ctx_attn_kernel(q_ref, k_ref, v_ref, o_ref):
    scale = NA_DH ** -0.5
    for h in range(NA_HEADS):
        sl = slice(h * NA_DH, (h + 1) * NA_DH)
        s = lax.dot_general(q_ref[0, :, sl], k_ref[0, :, sl], NT_DIMS, preferred_element_type=F32) * scale
        o_ref[0, :, sl] = _softmax_pv([s], [v_ref[0, :, sl]]).astype(o_ref.dtype)


def _ctx_attn_call(q, k, v):
    b, c, _ = q.shape
    spec = pl.BlockSpec((1, c, D_ATT), lambda i: (i, 0, 0))
    return pl.pallas_call(
        _ctx_attn_kernel,
        grid=(b,),
        in_specs=[spec, spec, spec],
        out_specs=spec,
        out_shape=jax.ShapeDtypeStruct((b, c, D_ATT), BF16),
        compiler_params=_params(("parallel",)),
        name="context_attention",
    )(q, k, v)


def _conv_kernel(prev_ref, cur_ref, next_ref, w_ref, b_ref, g_ref, bb_ref, o_ref, buf_ref):
    i = pl.program_id(1)
    tl = cur_ref.shape[1]
    buf_ref[0, 0:HALO] = jnp.where(i > 0, prev_ref[0], 0.0)
    buf_ref[0, HALO:HALO + tl] = cur_ref[0]
    buf_ref[0, HALO + tl:2 * HALO + tl] = jnp.where(i < pl.num_programs(1) - 1, next_ref[0], 0.0)
    span = tl + 2 * HALO - SUBLANES
    for r in range(1, SUBLANES):
        buf_ref[r, 0:span] = buf_ref[0, r:r + span]
    rows = 32
    base = HALO - CONV_PAD
    for r0 in range(0, tl, rows):
        acc = jnp.zeros((rows, D_CONV), F32) + b_ref[...]
        for k in range(CONV_W):
            q, r = divmod(base + k, SUBLANES)
            start = r0 + q * SUBLANES
            acc = acc + buf_ref[r, start:start + rows, :] * w_ref[k:k + 1, :]
        mu = jnp.mean(acc, axis=-1, keepdims=True)
        cen = acc - mu
        var = jnp.mean(cen * cen, axis=-1, keepdims=True)
        y = cen * lax.rsqrt(var + EPS) * g_ref[...] + bb_ref[...]
        o_ref[0, r0:r0 + rows, :] = (y * _sigmoid(y)).astype(o_ref.dtype)


def _conv_call(glu, dw_w, dw_b, ln_g, ln_b):
    b, l, c = glu.shape
    tl = min(l, 128)
    hb = tl // HALO
    last_halo = l // HALO - 1
    vec = pl.BlockSpec((1, c), lambda i, j: (0, 0))
    return pl.pallas_call(
        _conv_kernel,
        grid=(b, l // tl),
        in_specs=[pl.BlockSpec((1, HALO, c), lambda i, j: (i, jnp.maximum(j * hb - 1, 0), 0)),
                  pl.BlockSpec((1, tl, c), lambda i, j: (i, j, 0)),
                  pl.BlockSpec((1, HALO, c), lambda i, j: (i, jnp.minimum((j + 1) * hb, last_halo), 0)),
                  pl.BlockSpec((CONV_W, c), lambda i, j: (0, 0)),
                  vec, vec, vec],
        out_specs=pl.BlockSpec((1, tl, c), lambda i, j: (i, j, 0)),
        out_shape=jax.ShapeDtypeStruct((b, l, c), BF16),
        scratch_shapes=[pltpu.VMEM((SUBLANES, tl + 2 * HALO, c), F32)],
        compiler_params=_params(("parallel", "arbitrary")),
        name="conformer_conv",
    )(glu, glu, glu, dw_w, dw_b, ln_g, ln_b)


def _merge_kernel(cp_ref, na_ref, sgc_ref, sgn_ref, wc_ref, wn_ref, y_ref):
    conv = jnp.dot(cp_ref[...], wc_ref[...], preferred_element_type=F32)
    na = jnp.dot(na_ref[...], wn_ref[...], preferred_element_type=F32)
    y_ref[...] = (sgc_ref[...] * conv + sgn_ref[...] * na).astype(y_ref.dtype)


def _merge_call(cpre, na, sg, w_conv_out, w_na_out):
    t, kc = cpre.shape
    d = w_conv_out.shape[1]
    tm = min(t, 512)
    return pl.pallas_call(
        _merge_kernel,
        grid=(t // tm,),
        in_specs=[pl.BlockSpec((tm, kc), lambda i: (i, 0)),
                  pl.BlockSpec((tm, kc), lambda i: (i, 0)),
                  pl.BlockSpec((tm, d), lambda i: (i, 0)),
                  pl.BlockSpec((tm, d), lambda i: (i, 1)),
                  pl.BlockSpec((kc, d), lambda i: (0, 0)),
                  pl.BlockSpec((kc, d), lambda i: (0, 0))],
        out_specs=pl.BlockSpec((tm, d), lambda i: (i, 0)),
        out_shape=jax.ShapeDtypeStruct((t, d), BF16),
        compiler_params=_params(("parallel",)),
        name="branch_merge",
    )(cpre, na, sg, sg, w_conv_out, w_na_out)


def _oproj_kernel(y_ref, w_ref, x_ref, g_ref, o_ref):
    o_ref[0] = x_ref[0] + g_ref[0] * jnp.dot(y_ref[0], w_ref[...], preferred_element_type=F32)


def _oproj_call(y, w_o, x, gate):
    b, l, d = x.shape
    tm = min(l, 512)
    return pl.pallas_call(
        _oproj_kernel,
        grid=(b, l // tm),
        in_specs=[pl.BlockSpec((1, tm, d), lambda i, j: (i, j, 0)),
                  pl.BlockSpec((d, d), lambda i, j: (0, 0)),
                  pl.BlockSpec((1, tm, d), lambda i, j: (i, j, 0)),
                  pl.BlockSpec((1, 1, d), lambda i, j: (i, 0, 0))],
        out_specs=pl.BlockSpec((1, tm, d), lambda i, j: (i, j, 0)),
        out_shape=jax.ShapeDtypeStruct((b, l, d), F32),
        compiler_params=_params(("parallel", "parallel")),
        name="out_proj_residual",
    )(y, w_o, x, gate)


def _peer_scores_kernel(h_ref, w_ref, sk_ref, o_ref):
    q = jnp.dot(h_ref[...], w_ref[...], preferred_element_type=F32)
    for g in range(2 * P_HEADS):
        qg = q[:, g * D_KEY_HALF:(g + 1) * D_KEY_HALF]
        ms = jnp.mean(qg * qg, axis=-1, keepdims=True)
        qn = (qg * lax.rsqrt(ms + EPS)).astype(BF16)
        o_ref[g] = lax.dot_general(sk_ref[g], qn, NT_DIMS, preferred_element_type=F32)


def _peer_scores_call(h, w_pq, sub_keys):
    t, d = h.shape
    ng = 2 * P_HEADS
    tm = min(t, 256)
    return pl.pallas_call(
        _peer_scores_kernel,
        grid=(t // tm,),
        in_specs=[pl.BlockSpec((tm, d), lambda i: (i, 0)),
                  pl.BlockSpec((d, ng * D_KEY_HALF), lambda i: (0, 0)),
                  pl.BlockSpec((ng, N_KEYS, D_KEY_HALF), lambda i: (0, 0, 0))],
        out_specs=pl.BlockSpec((ng, N_KEYS, tm), lambda i: (0, 0, i)),
        out_shape=jax.ShapeDtypeStruct((ng, N_KEYS, t), F32),
        compiler_params=_params(("parallel",)),
        name="peer_scores",
    )(h, w_pq, sub_keys)


def _top16_rows(s, iota, tie_break, want_rank):
    rank = jnp.full(s.shape, float(PK_TOPK), F32) if want_rank else None
    vals = []
    for k in range(PK_TOPK):
        m = jnp.max(s, axis=0, keepdims=True)
        if tie_break:
            idx = jnp.min(jnp.where(s == m, iota, float(N_KEYS)), axis=0, keepdims=True)
            sel = iota == idx
        else:
            sel = s == m
        if want_rank:
            rank = jnp.where(sel, float(k), rank)
        s = jnp.where(sel, -jnp.inf, s)
        vals.append(m)
    taken = jnp.sum(jnp.where(s == -jnp.inf, 1.0, 0.0), axis=0, keepdims=True)
    return vals, rank, taken


def _peer_topk_kernel(s_ref, rank2_ref, cnt_ref, e2_ref, e1_ref):
    width = s_ref.shape[2]
    n_cand = 80
    r = lax.broadcasted_iota(jnp.int32, (n_cand, width), 0)
    ci = jnp.where(r < 16, 0, jnp.where(r < 72, 1 + ((r - 16) >> 3), r - 64))
    cj = jnp.where(r < 16, r, jnp.where(r < 72, (r - 16) & 7, 0))
    valid = (ci + 1) * (cj + 1) <= PK_TOPK
    pos = (ci * PK_TOPK + cj).astype(F32)
    iota = lax.broadcasted_iota(jnp.int32, (N_KEYS, width), 0).astype(F32)

    def select(ls, tie_break):
        s1 = s_ref[0, :, ls]
        s2 = s_ref[1, :, ls]
        v1, rank1, taken1 = _top16_rows(s1, iota, tie_break, tie_break)
        v2, rank2, taken2 = _top16_rows(s2, iota, tie_break, True)
        v2_all = jnp.concatenate(v2, axis=0)
        blocks = [v1[0] + v2_all]
        for i in range(1, 8):
            blocks.append(v1[i] + v2_all[0:8])
        blocks.append(jnp.concatenate(v1[8:], axis=0) + v2[0])
        cand = jnp.where(valid, jnp.concatenate(blocks, axis=0), -jnp.inf)
        selm = jnp.zeros((n_cand, width), F32)
        top = None
        z = None
        for k in range(PK_TOPK):
            m = jnp.max(cand, axis=0, keepdims=True)
            if tie_break:
                pmin = jnp.min(jnp.where(cand == m, pos, 1e9), axis=0, keepdims=True)
                sel = pos == pmin
            else:
                sel = cand == m
            selm = jnp.where(sel, 1.0, selm)
            cand = jnp.where(sel, -jnp.inf, cand)
            if k == 0:
                top = m
                z = jnp.ones_like(m)
            else:
                z = z + jnp.exp(m - top)
        counts = [jnp.sum(selm[0:16], axis=0, keepdims=True)]
        for i in range(1, 8):
            counts.append(jnp.sum(selm[8 + 8 * i:16 + 8 * i], axis=0, keepdims=True))
        for q in range(8):
            counts.append(selm[72 + q:73 + q])
        cnt = jnp.zeros((N_KEYS, width), F32)
        for i in range(PK_TOPK):
            hit = (rank1 == float(i)) if tie_break else (s1 == v1[i])
            cnt = jnp.where(hit, counts[i], cnt)
        rank2_ref[0, :, ls] = rank2.astype(rank2_ref.dtype)
        cnt_ref[0, :, ls] = cnt
        e2_ref[0, :, ls] = jnp.exp(s2 - v2[0]).astype(e2_ref.dtype)
        e1_ref[0, :, ls] = jnp.exp(s1 - v1[0]) / z
        taken = counts[0]
        for cnt_i in counts[1:]:
            taken = taken + cnt_i
        return taken1 + taken2 + taken

    ls = slice(None)
    taken = select(ls, tie_break=False)

    @pl.when(jnp.max(taken) > 3.0 * PK_TOPK)
    def _():
        select(ls, tie_break=True)


def _peer_topk_call(scores):
    ng, nk, t = scores.shape
    tt = min(t, 512)
    out_f = jax.ShapeDtypeStruct((P_HEADS, nk, t), F32)
    out_b = jax.ShapeDtypeStruct((P_HEADS, nk, t), BF16)
    ospec = pl.BlockSpec((1, nk, tt), lambda i, h: (h, 0, i))
    return pl.pallas_call(
        _peer_topk_kernel,
        grid=(t // tt, P_HEADS),
        in_specs=[pl.BlockSpec((2, nk, tt), lambda i, h: (h, 0, i))],
        out_specs=[ospec, ospec, ospec, ospec],
        out_shape=[out_b, out_f, out_b, out_f],
        compiler_params=_params(("parallel", "parallel")),
        name="peer_topk",
    )(scores)


BF16_ROWS = 16


def _peer_dense_kernel(h_ref, rank2_ref, cnt_ref, e2_ref, e1_ref, wd_ref, wu_ref, x_ref, g_ref, o_ref,
                       acc_ref, m_ref):
    j = pl.program_id(1)
    ce, tt = m_ref.shape
    groups = N_KEYS // BF16_ROWS

    @pl.when(j == 0)
    def _():
        acc_ref[...] = jnp.zeros_like(acc_ref)

    a_t = lax.dot_general(wd_ref[...], h_ref[...], NT_DIMS, preferred_element_type=F32)
    for ai in range(ce // N_KEYS):
        rows = slice(ai * N_KEYS, (ai + 1) * N_KEYS)
        g = None
        for h in range(P_HEADS):
            cn = jnp.broadcast_to(cnt_ref[h, ai:ai + 1, :], (BF16_ROWS, tt)).astype(BF16)[None]
            e1 = jnp.broadcast_to(e1_ref[h, ai:ai + 1, :], (BF16_ROWS, tt)).astype(BF16)[None]
            r2 = rank2_ref[h].reshape(groups, BF16_ROWS, tt)
            e2 = e2_ref[h].reshape(groups, BF16_ROWS, tt)
            term = jnp.where(r2 < cn, e2 * e1, jnp.zeros_like(e2))
            g = term if g is None else g + term
        at = a_t[rows, :]
        act = (at * (lax.erf(at * (1.0 / math.sqrt(2.0))) + 1.0) * 0.5).astype(BF16)
        m_ref[rows, :] = g.reshape(N_KEYS, tt) * act
    acc_ref[...] += jnp.dot(wu_ref[...], m_ref[...], preferred_element_type=F32)

    @pl.when(j == pl.num_programs(1) - 1)
    def _():
        o_ref[...] = x_ref[...] + g_ref[0] * acc_ref[...].T


def _peer_dense_call(h, rank2, cnt, e2, e1, w_down, w_up_t, x, gate):
    t, d = h.shape
    e = w_down.shape[0]
    tt = 512 if t % 512 == 0 else 256
    ce = 1024
    per_step = ce // N_KEYS
    tiles_per_gate = t // (gate.shape[0] * tt)
    cnt = cnt.reshape(P_HEADS, e // ce, per_step, t)
    e1 = e1.reshape(P_HEADS, e // ce, per_step, t)
    once = dict(pipeline_mode=pl.Buffered(1))
    sel_spec = pl.BlockSpec((P_HEADS, N_KEYS, tt), lambda i, j: (0, 0, i), **once)
    row_spec = pl.BlockSpec((P_HEADS, None, per_step, tt), lambda i, j: (0, j, 0, i))
    return pl.pallas_call(
        _peer_dense_kernel,
        grid=(t // tt, e // ce),
        in_specs=[pl.BlockSpec((tt, d), lambda i, j: (i, 0), **once),
                  sel_spec, row_spec, sel_spec, row_spec,
                  pl.BlockSpec((ce, d), lambda i, j: (j, 0)),
                  pl.BlockSpec((d, ce), lambda i, j: (0, j)),
                  pl.BlockSpec((tt, d), lambda i, j: (i, 0), **once),
                  pl.BlockSpec((1, 1, d), lambda i, j: (i // tiles_per_gate, 0, 0))],
        out_specs=pl.BlockSpec((tt, d), lambda i, j: (i, 0)),
        out_shape=jax.ShapeDtypeStruct((t, d), F32),
        scratch_shapes=[pltpu.VMEM((d, tt), F32), pltpu.VMEM((ce, tt), BF16)],
        compiler_params=_params(("parallel", "arbitrary"), 60 * 1024 * 1024),
        name="peer_dense",
    )(h, rank2, cnt, e2, e1, w_down, w_up_t, x, gate)


def _peer_block(x, norm_g, shift, scale, gate, w_pq, sub_keys, w_down, w_up_t):
    b, l, d = x.shape
    h = _normmod_call(x, norm_g, shift, scale).reshape(b * l, d)
    scores = _peer_scores_call(h, w_pq, sub_keys)
    rank2, cnt, e2, e1 = _peer_topk_call(scores)
    out = _peer_dense_call(h, rank2, cnt, e2, e1, w_down, w_up_t, x.reshape(b * l, d), gate)
    return out.reshape(b, l, d)


def kernel(x, c, ctx, c_ctx, w_mod, b_mod, norm1_g, w_in, dw_w, dw_b, cln_g, cln_b, w_conv_out,
           qn_g, kn_g, rpb, w_na_out, w_o, norm2_g, w_pq, sub_keys, w_down, w_up):
    bsz, t, d = x.shape
    n_ctx = ctx.shape[1]
    depth = w_mod.shape[0]
    off_q = 2 * D_CONV
    off_k = off_q + D_ATT
    off_v = off_k + D_ATT
    off_gate = off_v + D_ATT

    n_rows_mod = 8
    cvec = jnp.concatenate([c, c_ctx[None], jnp.zeros((n_rows_mod - bsz - 1, d), F32)], axis=0)
    mod = _mod_call(cvec, w_mod, b_mod[:, None, :])

    cx = ctx
    for l in range(depth):
        last = l == depth - 1
        mod_l = mod[l, :bsz].reshape(bsz, 1, 6, d)
        mod_c = jnp.broadcast_to(mod[l, bsz].reshape(1, 1, 6, d), (bsz, 1, 6, d))
        sh1, sc1, g1, sh2, sc2, g2 = [mod_l[:, :, i] for i in range(6)]
        csh1, csc1, cg1, csh2, csc2, cg2 = [mod_c[:, :, i] for i in range(6)]

        w_in_l = w_in[l]
        w_glu_a = w_in_l[:, :D_CONV].astype(BF16)
        w_glu_g = w_in_l[:, D_CONV:off_q].astype(BF16)
        w_q = w_in_l[:, off_q:off_k].astype(BF16)
        w_k = w_in_l[:, off_k:off_v].astype(BF16)
        w_v = w_in_l[:, off_v:off_gate].astype(BF16)
        w_gate = w_in_l[:, off_gate:].astype(BF16)
        qg_vec = jnp.tile(qn_g[l], NA_HEADS)[None]
        kg_vec = jnp.tile(kn_g[l], NA_HEADS)[None]
        w_conv_l = w_conv_out[l].astype(BF16)
        w_na_l = w_na_out[l].astype(BF16)
        w_o_l = w_o[l].astype(BF16)
        n1g = norm1_g[l][None]
        conv_vecs = (dw_w[l], dw_b[l][None], cln_g[l][None], cln_b[l][None])

        def mixer_inputs(z, shift, scale, need_q):
            b_, l_, _ = z.shape
            h = _normmod_call(z, n1g, shift, scale).reshape(b_ * l_, d)
            k_ = _mm_call(_mm_headnorm_kernel, h, [w_k], [kg_vec], D_ATT, BF16, "proj_k").reshape(b_, l_, D_ATT)
            v_ = _mm_call(_mm_plain_kernel, h, [w_v], [], D_ATT, BF16, "proj_v").reshape(b_, l_, D_ATT)
            if not need_q:
                return h, None, k_, v_
            q_ = _mm_call(_mm_headnorm_kernel, h, [w_q], [qg_vec], D_ATT, BF16, "proj_q").reshape(b_, l_, D_ATT)
            return h, q_, k_, v_

        def mixer_tail(z, h, na, gate):
            b_, l_, _ = z.shape
            glu = _mm_call(_mm_glu_kernel, h, [w_glu_a, w_glu_g], [], D_CONV, F32, "proj_glu")
            sg = _mm_call(_mm_sigmoid_kernel, h, [w_gate], [], 2 * d, F32, "proj_gates")
            cpre = _conv_call(glu.reshape(b_, l_, D_CONV), *conv_vecs).reshape(b_ * l_, D_CONV)
            y = _merge_call(cpre, na.reshape(b_ * l_, D_ATT), sg, w_conv_l, w_na_l)
            return _oproj_call(y.reshape(b_, l_, d), w_o_l, z, gate)

        h_c, q_c, k_c, v_c = mixer_inputs(cx, csh1, csc1, not last)
        h_l, q_l, k_l, v_l = mixer_inputs(x, sh1, sc1, True)
        bias = _na_bias_table(rpb[l])
        na_l = _na_call(q_l, k_l, v_l, k_c, v_c, bias)
        x = mixer_tail(x, h_l, na_l, g1)
        if not last:
            na_c = _ctx_attn_call(q_c, k_c, v_c)
            cx = mixer_tail(cx, h_c, na_c, cg1)

        w_pq_l = w_pq[l].astype(BF16)
        sk_l = sub_keys[l].reshape(2 * P_HEADS, N_KEYS, D_KEY_HALF).astype(BF16)
        w_down_l = w_down[l].astype(BF16)
        w_up_t_l = w_up[l].astype(BF16).T
        n2g = norm2_g[l][None]
        x = _peer_block(x, n2g, sh2, sc2, g2, w_pq_l, sk_l, w_down_l, w_up_t_l)
        if not last:
            cx = _peer_block(cx, n2g, csh2, csc2, cg2[:1], w_pq_l, sk_l, w_down_l, w_up_t_l)
    return x
```

```python
import functools
import math

import numpy as np
import jax
import jax.numpy as jnp
from jax import lax
from jax.experimental import pallas as pl
from jax.experimental.pallas import tpu as pltpu

F32 = jnp.float32
BF16 = jnp.bfloat16

GRID_W = 64
D_CONV = 1024
CONV_W = 31
CONV_PAD = CONV_W // 2
NA_HEADS = 8
NA_DH = 128
D_ATT = NA_HEADS * NA_DH
NA_KH = 8
NA_KW = 16
P_HEADS = 8
N_KEYS = 128
D_KEY_HALF = 128
PK_TOPK = 16
EPS = 1e-6

LANES = 128
SUBLANES = 8
HALO = 16
NA_ROWS_PER_STEP = 4
NA_BAND_BLOCKS = 3
MASK_VALUE = -1e30
VMEM_LIMIT = 56 * 1024 * 1024

NT_DIMS = (((1,), (1,)), ((), ()))


def _params(sem, vmem=VMEM_LIMIT):
    return pltpu.CompilerParams(dimension_semantics=sem, vmem_limit_bytes=vmem)


def _sigmoid(x):
    return 1.0 / (1.0 + jnp.exp(-x))


def _mod_kernel(c_ref, w_ref, b_ref, o_ref):
    c = c_ref[...]
    s = (c * _sigmoid(c)).astype(BF16)
    o_ref[0] = jnp.dot(s, w_ref[0].astype(BF16), preferred_element_type=F32) + b_ref[0]


def _mod_call(cvec, w_mod, b_mod):
    nl, d, n = w_mod.shape
    r = cvec.shape[0]
    tn = 512
    return pl.pallas_call(
        _mod_kernel,
        grid=(nl, n // tn),
        in_specs=[pl.BlockSpec((r, d), lambda l, j: (0, 0)),
                  pl.BlockSpec((1, d, tn), lambda l, j: (l, 0, j)),
                  pl.BlockSpec((1, 1, tn), lambda l, j: (l, 0, j))],
        out_specs=pl.BlockSpec((1, r, tn), lambda l, j: (l, 0, j)),
        out_shape=jax.ShapeDtypeStruct((nl, r, n), F32),
        compiler_params=_params(("parallel", "parallel")),
        name="adaln_mod",
    )(cvec, w_mod, b_mod)


def _normmod(x, g, shift, scale):
    ms = jnp.mean(x * x, axis=-1, keepdims=True)
    n = x * lax.rsqrt(ms + EPS) * g
    return (n * (1.0 + scale) + shift).astype(BF16)


def _normmod_kernel(x_ref, g_ref, sh_ref, sc_ref, o_ref):
    o_ref[0] = _normmod(x_ref[0], g_ref[...], sh_ref[0], sc_ref[0])


def _normmod_call(x, g, shift, scale):
    b, l, d = x.shape
    tl = min(l, 512)
    return pl.pallas_call(
        _normmod_kernel,
        grid=(b, l // tl),
        in_specs=[pl.BlockSpec((1, tl, d), lambda i, j: (i, j, 0)),
                  pl.BlockSpec((1, d), lambda i, j: (0, 0)),
                  pl.BlockSpec((1, 1, d), lambda i, j: (i, 0, 0)),
                  pl.BlockSpec((1, 1, d), lambda i, j: (i, 0, 0))],
        out_specs=pl.BlockSpec((1, tl, d), lambda i, j: (i, j, 0)),
        out_shape=jax.ShapeDtypeStruct((b, l, d), BF16),
        compiler_params=_params(("parallel", "parallel")),
        name="normmod",
    )(x, g, shift, scale)


def _mm_glu_kernel(h_ref, wa_ref, wg_ref, o_ref):
    h = h_ref[...]
    a = jnp.dot(h, wa_ref[...], preferred_element_type=F32)
    g = jnp.dot(h, wg_ref[...], preferred_element_type=F32)
    o_ref[...] = (a * _sigmoid(g)).astype(o_ref.dtype)


def _mm_headnorm_kernel(h_ref, w_ref, g_ref, o_ref):
    y = jnp.dot(h_ref[...], w_ref[...], preferred_element_type=F32)
    for k in range(y.shape[1] // LANES):
        sl = slice(k * LANES, (k + 1) * LANES)
        yk = y[:, sl]
        ms = jnp.mean(yk * yk, axis=-1, keepdims=True)
        o_ref[:, sl] = (yk * lax.rsqrt(ms + EPS) * g_ref[:, sl]).astype(o_ref.dtype)


def _mm_plain_kernel(h_ref, w_ref, o_ref):
    o_ref[...] = jnp.dot(h_ref[...], w_ref[...], preferred_element_type=F32).astype(o_ref.dtype)


def _mm_sigmoid_kernel(h_ref, w_ref, o_ref):
    y = jnp.dot(h_ref[...], w_ref[...], preferred_element_type=F32)
    o_ref[...] = _sigmoid(y).astype(o_ref.dtype)


def _mm_call(kernel, h, ws, extras, n_out, out_dtype, name):
    t, k = h.shape
    tm = min(t, 1024)
    tn = 512
    in_specs = [pl.BlockSpec((tm, k), lambda i, j: (i, 0))]
    in_specs += [pl.BlockSpec((k, tn), lambda i, j: (0, j)) for _ in ws]
    in_specs += [pl.BlockSpec((1, tn), lambda i, j: (0, j)) for _ in extras]
    return pl.pallas_call(
        kernel,
        grid=(t // tm, n_out // tn),
        in_specs=in_specs,
        out_specs=pl.BlockSpec((tm, tn), lambda i, j: (i, j)),
        out_shape=jax.ShapeDtypeStruct((t, n_out), out_dtype),
        compiler_params=_params(("parallel", "parallel")),
        name=name,
    )(h, *ws, *extras)


def _softmax_pv(s_list, v_list):
    m = s_list[0].max(axis=-1, keepdims=True)
    for s in s_list[1:]:
        m = jnp.maximum(m, s.max(axis=-1, keepdims=True))
    num = None
    den = None
    for s, v in zip(s_list, v_list):
        p = jnp.exp(s - m)
        d = p.sum(axis=-1, keepdims=True)
        o = jnp.dot(p.astype(BF16), v, preferred_element_type=F32)
        num = o if num is None else num + o
        den = d if den is None else den + d
    return num / den


def _na_kernel(q_ref, k0_ref, k1_ref, k2_ref, v0_ref, v1_ref, v2_ref, kc_ref, vc_ref, bias_ref, o_ref):
    scale = NA_DH ** -0.5
    for h in range(NA_HEADS):
        sl = slice(h * NA_DH, (h + 1) * NA_DH)
        q = q_ref[0, :, sl]
        kb = jnp.concatenate([k0_ref[0, :, sl], k1_ref[0, :, sl], k2_ref[0, :, sl]], axis=0)
        vb = jnp.concatenate([v0_ref[0, :, sl], v1_ref[0, :, sl], v2_ref[0, :, sl]], axis=0)
        s_w = lax.dot_general(q, kb, NT_DIMS, preferred_element_type=F32) * scale + bias_ref[0, h]
        s_c = lax.dot_general(q, kc_ref[0, :, sl], NT_DIMS, preferred_element_type=F32) * scale
        o = _softmax_pv([s_w, s_c], [vb, vc_ref[0, :, sl]])
        o_ref[0, :, sl] = o.astype(o_ref.dtype)


def _na_bias_table(rpb_l):
    r4 = NA_ROWS_PER_STEP
    band_rows = NA_BAND_BLOCKS * r4
    n_dr, n_dc = 2 * NA_KH - 1, 2 * NA_KW - 1
    qc = np.arange(GRID_W)[:, None]
    kc = np.arange(GRID_W)[None, :]
    cs = np.clip(qc - NA_KW // 2, 0, GRID_W - NA_KW)
    col_ok = (kc >= cs) & (kc < cs + NA_KW)
    col_sel = (np.arange(n_dc)[:, None, None] == (kc - qc + NA_KW - 1)[None]) & col_ok[None]
    qr = np.arange(r4)[:, None]
    kr = np.arange(band_rows)[None, :]
    row_sel, row_ok = [], []
    for off, ks in ((0, np.zeros_like(qr)), (r4, qr), (band_rows - r4, np.full_like(qr, band_rows - NA_KH))):
        ok = (kr >= ks) & (kr < ks + NA_KH)
        row_ok.append(ok)
        row_sel.append((np.arange(n_dr)[None, None, :] == (kr - off - qr + NA_KH - 1)[:, :, None]) & ok[:, :, None])
    row_sel = jnp.asarray(np.stack(row_sel), F32)
    col_sel = jnp.asarray(col_sel, F32)
    bias = jnp.einsum("vqkr,hrc,cxy->vhqxky", row_sel, rpb_l, col_sel, precision=lax.Precision.HIGHEST)
    ok = np.stack(row_ok)[:, None, :, None, :, None] & col_ok[None, None, None, :, None, :]
    bias = jnp.where(ok, bias, MASK_VALUE)
    return bias.reshape(3, NA_HEADS, r4 * GRID_W, band_rows * GRID_W)


def _na_call(q, k, v, kc, vc, bias):
    b, t, _ = q.shape
    c = kc.shape[1]
    tq = NA_ROWS_PER_STEP * GRID_W
    nb = t // tq
    assert nb >= NA_BAND_BLOCKS + 1 and NA_KH == 2 * NA_ROWS_PER_STEP

    def band(d):
        return pl.BlockSpec((1, tq, D_ATT), lambda i, j: (i, jnp.clip(j - 1, 0, nb - NA_BAND_BLOCKS) + d, 0))

    def variant(i, j):
        return (jnp.where(j == 0, 0, jnp.where(j == nb - 1, 2, 1)), 0, 0, 0)

    return pl.pallas_call(
        _na_kernel,
        grid=(b, nb),
        in_specs=[pl.BlockSpec((1, tq, D_ATT), lambda i, j: (i, j, 0)),
                  band(0), band(1), band(2), band(0), band(1), band(2),
                  pl.BlockSpec((1, c, D_ATT), lambda i, j: (i, 0, 0)),
                  pl.BlockSpec((1, c, D_ATT), lambda i, j: (i, 0, 0)),
                  pl.BlockSpec((1, NA_HEADS, tq, NA_BAND_BLOCKS * tq), variant)],
        out_specs=pl.BlockSpec((1, tq, D_ATT), lambda i, j: (i, j, 0)),
        out_shape=jax.ShapeDtypeStruct((b, t, D_ATT), BF16),
        compiler_params=_params(("parallel", "arbitrary")),
        name="neighborhood_attention",
    )(q, k, k, k, v, v, v, kc, vc, bias)


def _ctx_attn_kernel(q_ref, k_ref, v_ref, o_ref):
    scale = NA_DH ** -0.5
    for h in range(NA_HEADS):
        sl = slice(h * NA_DH, (h + 1) * NA_DH)
        s = lax.dot_general(q_ref[0, :, sl], k_ref[0, :, sl], NT_DIMS, preferred_element_type=F32) * scale
        o_ref[0, :, sl] = _softmax_pv([s], [v_ref[0, :, sl]]).astype(o_ref.dtype)


def _ctx_attn_call(q, k, v):
    b, c, _ = q.shape
    spec = pl.BlockSpec((1, c, D_ATT), lambda i: (i, 0, 0))
    return pl.pallas_call(
        _ctx_attn_kernel,
        grid=(b,),
        in_specs=[spec, spec, spec],
        out_specs=spec,
        out_shape=jax.ShapeDtypeStruct((b, c, D_ATT), BF16),
        compiler_params=_params(("parallel",)),
        name="context_attention",
    )(q, k, v)


def _conv_kernel(prev_ref, cur_ref, next_ref, w_ref, b_ref, g_ref, bb_ref, o_ref, buf_ref):
    i = pl.program_id(1)
    tl = cur_ref.shape[1]
    buf_ref[0, 0:HALO] = jnp.where(i > 0, prev_ref[0], 0.0)
    buf_ref[0, HALO:HALO + tl] = cur_ref[0]
    buf_ref[0, HALO + tl:2 * HALO + tl] = jnp.where(i < pl.num_programs(1) - 1, next_ref[0], 0.0)
    span = tl + 2 * HALO - SUBLANES
    for r in range(1, SUBLANES):
        buf_ref[r, 0:span] = buf_ref[0, r:r + span]
    rows = 32
    base = HALO - CONV_PAD
    for r0 in range(0, tl, rows):
        acc = jnp.zeros((rows, D_CONV), F32) + b_ref[...]
        for k in range(CONV_W):
            q, r = divmod(base + k, SUBLANES)
            start = r0 + q * SUBLANES
            acc = acc + buf_ref[r, start:start + rows, :] * w_ref[k:k + 1, :]
        mu = jnp.mean(acc, axis=-1, keepdims=True)
        cen = acc - mu
        var = jnp.mean(cen * cen, axis=-1, keepdims=True)
        y = cen * lax.rsqrt(var + EPS) * g_ref[...] + bb_ref[...]
        o_ref[0, r0:r0 + rows, :] = (y * _sigmoid(y)).astype(o_ref.dtype)


def _conv_call(glu, dw_w, dw_b, ln_g, ln_b):
    b, l, c = glu.shape
    tl = min(l, 128)
    hb = tl // HALO
    last_halo = l // HALO - 1
    vec = pl.BlockSpec((1, c), lambda i, j: (0, 0))
    return pl.pallas_call(
        _conv_kernel,
        grid=(b, l // tl),
        in_specs=[pl.BlockSpec((1, HALO, c), lambda i, j: (i, jnp.maximum(j * hb - 1, 0), 0)),
                  pl.BlockSpec((1, tl, c), lambda i, j: (i, j, 0)),
                  pl.BlockSpec((1, HALO, c), lambda i, j: (i, jnp.minimum((j + 1) * hb, last_halo), 0)),
                  pl.BlockSpec((CONV_W, c), lambda i, j: (0, 0)),
                  vec, vec, vec],
        out_specs=pl.BlockSpec((1, tl, c), lambda i, j: (i, j, 0)),
        out_shape=jax.ShapeDtypeStruct((b, l, c), BF16),
        scratch_shapes=[pltpu.VMEM((SUBLANES, tl + 2 * HALO, c), F32)],
        compiler_params=_params(("parallel", "arbitrary")),
        name="conformer_conv",
    )(glu, glu, glu, dw_w, dw_b, ln_g, ln_b)


def _merge_kernel(cp_ref, na_ref, sgc_ref, sgn_ref, wc_ref, wn_ref, y_ref):
    conv = jnp.dot(cp_ref[...], wc_ref[...], preferred_element_type=F32)
    na = jnp.dot(na_ref[...], wn_ref[...], preferred_element_type=F32)
    y_ref[...] = (sgc_ref[...] * conv + sgn_ref[...] * na).astype(y_ref.dtype)


def _merge_call(cpre, na, sg, w_conv_out, w_na_out):
    t, kc = cpre.shape
    d = w_conv_out.shape[1]
    tm = min(t, 512)
    return pl.pallas_call(
        _merge_kernel,
        grid=(t // tm,),
        in_specs=[pl.BlockSpec((tm, kc), lambda i: (i, 0)),
                  pl.BlockSpec((tm, kc), lambda i: (i, 0)),
                  pl.BlockSpec((tm, d), lambda i: (i, 0)),
                  pl.BlockSpec((tm, d), lambda i: (i, 1)),
                  pl.BlockSpec((kc, d), lambda i: (0, 0)),
                  pl.BlockSpec((kc, d), lambda i: (0, 0))],
        out_specs=pl.BlockSpec((tm, d), lambda i: (i, 0)),
        out_shape=jax.ShapeDtypeStruct((t, d), BF16),
        compiler_params=_params(("parallel",)),
        name="branch_merge",
    )(cpre, na, sg, sg, w_conv_out, w_na_out)


def _oproj_kernel(y_ref, w_ref, x_ref, g_ref, ng_ref, nsh_ref, nsc_ref, o_ref, h_ref):
    out = x_ref[0] + g_ref[0] * jnp.dot(y_ref[0], w_ref[...], preferred_element_type=F32)
    o_ref[0] = out
    h_ref[0] = _normmod(out, ng_ref[...], nsh_ref[0], nsc_ref[0])


def _oproj_call(y, w_o, x, gate, norm_g, shift, scale):
    b, l, d = x.shape
    tm = min(l, 512)
    tile = pl.BlockSpec((1, tm, d), lambda i, j: (i, j, 0))
    vec = pl.BlockSpec((1, 1, d), lambda i, j: (i, 0, 0))
    return pl.pallas_call(
        _oproj_kernel,
        grid=(b, l // tm),
        in_specs=[tile,
                  pl.BlockSpec((d, d), lambda i, j: (0, 0)),
                  tile, vec,
                  pl.BlockSpec((1, d), lambda i, j: (0, 0)),
                  vec, vec],
        out_specs=[tile, tile],
        out_shape=[jax.ShapeDtypeStruct((b, l, d), F32), jax.ShapeDtypeStruct((b, l, d), BF16)],
        compiler_params=_params(("parallel", "parallel")),
        name="out_proj_residual",
    )(y, w_o, x, gate, norm_g, shift, scale)


def _peer_scores_kernel(h_ref, w_ref, sk_ref, o_ref):
    q = jnp.dot(h_ref[...], w_ref[...], preferred_element_type=F32)
    for g in range(2 * P_HEADS):
        qg = q[:, g * D_KEY_HALF:(g + 1) * D_KEY_HALF]
        ms = jnp.mean(qg * qg, axis=-1, keepdims=True)
        qn = (qg * lax.rsqrt(ms + EPS)).astype(BF16)
        o_ref[g] = lax.dot_general(sk_ref[g], qn, NT_DIMS, preferred_element_type=F32)


def _peer_scores_call(h, w_pq, sub_keys):
    t, d = h.shape
    ng = 2 * P_HEADS
    tm = min(t, 256)
    return pl.pallas_call(
        _peer_scores_kernel,
        grid=(t // tm,),
        in_specs=[pl.BlockSpec((tm, d), lambda i: (i, 0)),
                  pl.BlockSpec((d, ng * D_KEY_HALF), lambda i: (0, 0)),
                  pl.BlockSpec((ng, N_KEYS, D_KEY_HALF), lambda i: (0, 0, 0))],
        out_specs=pl.BlockSpec((ng, N_KEYS, tm), lambda i: (0, 0, i)),
        out_shape=jax.ShapeDtypeStruct((ng, N_KEYS, t), F32),
        compiler_params=_params(("parallel",)),
        name="peer_scores",
    )(h, w_pq, sub_keys)


def _top16_rows(s, iota, tie_break, want_rank):
    rank = jnp.full(s.shape, float(PK_TOPK), F32) if want_rank else None
    vals = []
    for k in range(PK_TOPK):
        m = jnp.max(s, axis=0, keepdims=True)
        if tie_break:
            idx = jnp.min(jnp.where(s == m, iota, float(N_KEYS)), axis=0, keepdims=True)
            sel = iota == idx
        else:
            sel = s == m
        if want_rank:
            rank = jnp.where(sel, float(k), rank)
        s = jnp.where(sel, -jnp.inf, s)
        vals.append(m)
    taken = jnp.sum(jnp.where(s == -jnp.inf, 1.0, 0.0), axis=0, keepdims=True)
    return vals, rank, taken


def _peer_topk_kernel(s_ref, rank2_ref, cnt_ref, e2_ref, e1_ref):
    width = s_ref.shape[2]
    n_cand = 80
    r = lax.broadcasted_iota(jnp.int32, (n_cand, width), 0)
    ci = jnp.where(r < 16, 0, jnp.where(r < 72, 1 + ((r - 16) >> 3), r - 64))
    cj = jnp.where(r < 16, r, jnp.where(r < 72, (r - 16) & 7, 0))
    valid = (ci + 1) * (cj + 1) <= PK_TOPK
    pos = (ci * PK_TOPK + cj).astype(F32)
    iota = lax.broadcasted_iota(jnp.int32, (N_KEYS, width), 0).astype(F32)

    def select(ls, tie_break):
        s1 = s_ref[0, :, ls]
        s2 = s_ref[1, :, ls]
        v1, rank1, taken1 = _top16_rows(s1, iota, tie_break, tie_break)
        v2, rank2, taken2 = _top16_rows(s2, iota, tie_break, True)
        v2_all = jnp.concatenate(v2, axis=0)
        blocks = [v1[0] + v2_all]
        for i in range(1, 8):
            blocks.append(v1[i] + v2_all[0:8])
        blocks.append(jnp.concatenate(v1[8:], axis=0) + v2[0])
        cand = jnp.where(valid, jnp.concatenate(blocks, axis=0), -jnp.inf)
        selm = jnp.zeros((n_cand, width), F32)
        top = None
        z = None
        for k in range(PK_TOPK):
            m = jnp.max(cand, axis=0, keepdims=True)
            if tie_break:
                pmin = jnp.min(jnp.where(cand == m, pos, 1e9), axis=0, keepdims=True)
                sel = pos == pmin
            else:
                sel = cand == m
            selm = jnp.where(sel, 1.0, selm)
            cand = jnp.where(sel, -jnp.inf, cand)
            if k == 0:
                top = m
                z = jnp.ones_like(m)
            else:
                z = z + jnp.exp(m - top)
        counts = [jnp.sum(selm[0:16], axis=0, keepdims=True)]
        for i in range(1, 8):
            counts.append(jnp.sum(selm[8 + 8 * i:16 + 8 * i], axis=0, keepdims=True))
        for q in range(8):
            counts.append(selm[72 + q:73 + q])
        cnt = jnp.zeros((N_KEYS, width), F32)
        for i in range(PK_TOPK):
            hit = (rank1 == float(i)) if tie_break else (s1 == v1[i])
            cnt = jnp.where(hit, counts[i], cnt)
        rank2_ref[0, :, ls] = rank2.astype(rank2_ref.dtype)
        cnt_ref[0, :, ls] = cnt
        e2_ref[0, :, ls] = jnp.exp(s2 - v2[0]).astype(e2_ref.dtype)
        e1_ref[0, :, ls] = jnp.exp(s1 - v1[0]) / z
        taken = counts[0]
        for cnt_i in counts[1:]:
            taken = taken + cnt_i
        return taken1 + taken2 + taken

    ls = slice(None)
    taken = select(ls, tie_break=False)

    @pl.when(jnp.max(taken) > 3.0 * PK_TOPK)
    def _():
        select(ls, tie_break=True)


def _peer_topk_call(scores):
    ng, nk, t = scores.shape
    tt = min(t, 512)
    out_f = jax.ShapeDtypeStruct((P_HEADS, nk, t), F32)
    out_b = jax.ShapeDtypeStruct((P_HEADS, nk, t), BF16)
    ospec = pl.BlockSpec((1, nk, tt), lambda i, h: (h, 0, i))
    return pl.pallas_call(
        _peer_topk_kernel,
        grid=(t // tt, P_HEADS),
        in_specs=[pl.BlockSpec((2, nk, tt), lambda i, h: (h, 0, i))],
        out_specs=[ospec, ospec, ospec, ospec],
        out_shape=[out_b, out_f, out_b, out_f],
        compiler_params=_params(("parallel", "parallel")),
        name="peer_topk",
    )(scores)


BF16_ROWS = 16


def _peer_dense_kernel(h_ref, rank2_ref, cnt_ref, e2_ref, e1_ref, wd_ref, wu_ref, x_ref, g_ref, *rest, emit_next):
    if emit_next:
        ng_ref, nsh_ref, nsc_ref, o_ref, hn_ref, acc_ref, m_ref = rest
    else:
        o_ref, acc_ref, m_ref = rest
    j = pl.program_id(1)
    ce, tt = m_ref.shape
    groups = N_KEYS // BF16_ROWS

    @pl.when(j == 0)
    def _():
        acc_ref[...] = jnp.zeros_like(acc_ref)

    a_t = lax.dot_general(wd_ref[...], h_ref[...], NT_DIMS, preferred_element_type=F32)
    for ai in range(ce // N_KEYS):
        rows = slice(ai * N_KEYS, (ai + 1) * N_KEYS)
        g = None
        for h in range(P_HEADS):
            cn = jnp.broadcast_to(cnt_ref[h, ai:ai + 1, :], (BF16_ROWS, tt)).astype(BF16)[None]
            e1 = jnp.broadcast_to(e1_ref[h, ai:ai + 1, :], (BF16_ROWS, tt)).astype(BF16)[None]
            r2 = rank2_ref[h].reshape(groups, BF16_ROWS, tt)
            e2 = e2_ref[h].reshape(groups, BF16_ROWS, tt)
            term = jnp.where(r2 < cn, e2 * e1, jnp.zeros_like(e2))
            g = term if g is None else g + term
        at = a_t[rows, :]
        act = (at * (lax.erf(at * (1.0 / math.sqrt(2.0))) + 1.0) * 0.5).astype(BF16)
        m_ref[rows, :] = g.reshape(N_KEYS, tt) * act
    acc_ref[...] += jnp.dot(wu_ref[...], m_ref[...], preferred_element_type=F32)

    @pl.when(j == pl.num_programs(1) - 1)
    def _():
        out = x_ref[...] + g_ref[0] * acc_ref[...].T
        o_ref[...] = out
        if emit_next:
            hn_ref[...] = _normmod(out, ng_ref[...], nsh_ref[0], nsc_ref[0])


def _peer_dense_call(h, rank2, cnt, e2, e1, w_down, w_up_t, x, gate, next_norm):
    t, d = h.shape
    e = w_down.shape[0]
    tt = 512 if t % 512 == 0 else 256
    ce = 1024
    per_step = ce // N_KEYS
    tiles_per_gate = t // (gate.shape[0] * tt)
    cnt = cnt.reshape(P_HEADS, e // ce, per_step, t)
    e1 = e1.reshape(P_HEADS, e // ce, per_step, t)
    once = dict(pipeline_mode=pl.Buffered(1))
    sel_spec = pl.BlockSpec((P_HEADS, N_KEYS, tt), lambda i, j: (0, 0, i), **once)
    row_spec = pl.BlockSpec((P_HEADS, None, per_step, tt), lambda i, j: (0, j, 0, i))
    tile = pl.BlockSpec((tt, d), lambda i, j: (i, 0))
    vec = pl.BlockSpec((1, 1, d), lambda i, j: (i // tiles_per_gate, 0, 0))
    in_specs = [pl.BlockSpec((tt, d), lambda i, j: (i, 0), **once),
                sel_spec, row_spec, sel_spec, row_spec,
                pl.BlockSpec((ce, d), lambda i, j: (j, 0)),
                pl.BlockSpec((d, ce), lambda i, j: (0, j)),
                pl.BlockSpec((tt, d), lambda i, j: (i, 0), **once),
                vec]
    args = [h, rank2, cnt, e2, e1, w_down, w_up_t, x, gate]
    out_specs, out_shape = tile, jax.ShapeDtypeStruct((t, d), F32)
    if next_norm is not None:
        in_specs += [pl.BlockSpec((1, d), lambda i, j: (0, 0)), vec, vec]
        args += list(next_norm)
        out_specs, out_shape = [tile, tile], [out_shape, jax.ShapeDtypeStruct((t, d), BF16)]
    res = pl.pallas_call(
        functools.partial(_peer_dense_kernel, emit_next=next_norm is not None),
        grid=(t // tt, e // ce),
        in_specs=in_specs,
        out_specs=out_specs,
        out_shape=out_shape,
        scratch_shapes=[pltpu.VMEM((d, tt), F32), pltpu.VMEM((ce, tt), BF16)],
        compiler_params=_params(("parallel", "arbitrary"), 60 * 1024 * 1024),
        name="peer_dense",
    )(*args)
    return res if next_norm is not None else (res, None)


def _peer_block(x, h, gate, w_pq, sub_keys, w_down, w_up_t, next_norm):
    b, l, d = x.shape
    scores = _peer_scores_call(h, w_pq, sub_keys)
    rank2, cnt, e2, e1 = _peer_topk_call(scores)
    out, h_next = _peer_dense_call(h, rank2, cnt, e2, e1, w_down, w_up_t, x.reshape(b * l, d), gate, next_norm)
    return out.reshape(b, l, d), h_next


def kernel(x, c, ctx, c_ctx, w_mod, b_mod, norm1_g, w_in, dw_w, dw_b, cln_g, cln_b, w_conv_out,
           qn_g, kn_g, rpb, w_na_out, w_o, norm2_g, w_pq, sub_keys, w_down, w_up):
    bsz, t, d = x.shape
    n_ctx = ctx.shape[1]
    depth = w_mod.shape[0]
    off_q = 2 * D_CONV
    off_k = off_q + D_ATT
    off_v = off_k + D_ATT
    off_gate = off_v + D_ATT

    n_rows_mod = 8
    cvec = jnp.concatenate([c, c_ctx[None], jnp.zeros((n_rows_mod - bsz - 1, d), F32)], axis=0)
    mod = _mod_call(cvec, w_mod, b_mod[:, None, :])

    def layer_mods(l):
        mod_l = mod[l, :bsz].reshape(bsz, 1, 6, d)
        mod_c = jnp.broadcast_to(mod[l, bsz].reshape(1, 1, 6, d), (bsz, 1, 6, d))
        return [mod_l[:, :, i] for i in range(6)], [mod_c[:, :, i] for i in range(6)]

    cx = ctx
    (sh1, sc1, _, _, _, _), (csh1, csc1, _, _, _, _) = layer_mods(0)
    h_l = _normmod_call(x, norm1_g[0][None], sh1, sc1).reshape(bsz * t, d)
    h_c = _normmod_call(cx, norm1_g[0][None], csh1, csc1).reshape(bsz * n_ctx, d)
    for l in range(depth):
        last = l == depth - 1
        (_, _, g1, sh2, sc2, g2), (_, _, cg1, csh2, csc2, cg2) = layer_mods(l)

        w_in_l = w_in[l]
        w_glu_a = w_in_l[:, :D_CONV].astype(BF16)
        w_glu_g = w_in_l[:, D_CONV:off_q].astype(BF16)
        w_q = w_in_l[:, off_q:off_k].astype(BF16)
        w_k = w_in_l[:, off_k:off_v].astype(BF16)
        w_v = w_in_l[:, off_v:off_gate].astype(BF16)
        w_gate = w_in_l[:, off_gate:].astype(BF16)
        qg_vec = jnp.tile(qn_g[l], NA_HEADS)[None]
        kg_vec = jnp.tile(kn_g[l], NA_HEADS)[None]
        w_conv_l = w_conv_out[l].astype(BF16)
        w_na_l = w_na_out[l].astype(BF16)
        w_o_l = w_o[l].astype(BF16)
        n2g = norm2_g[l][None]
        conv_vecs = (dw_w[l], dw_b[l][None], cln_g[l][None], cln_b[l][None])

        def mixer_inputs(h, l_, need_q):
            b_ = h.shape[0] // l_
            k_ = _mm_call(_mm_headnorm_kernel, h, [w_k], [kg_vec], D_ATT, BF16, "proj_k").reshape(b_, l_, D_ATT)
            v_ = _mm_call(_mm_plain_kernel, h, [w_v], [], D_ATT, BF16, "proj_v").reshape(b_, l_, D_ATT)
            if not need_q:
                return None, k_, v_
            q_ = _mm_call(_mm_headnorm_kernel, h, [w_q], [qg_vec], D_ATT, BF16, "proj_q").reshape(b_, l_, D_ATT)
            return q_, k_, v_

        def mixer_tail(z, h, na, gate, shift2, scale2):
            b_, l_, _ = z.shape
            glu = _mm_call(_mm_glu_kernel, h, [w_glu_a, w_glu_g], [], D_CONV, F32, "proj_glu")
            sg = _mm_call(_mm_sigmoid_kernel, h, [w_gate], [], 2 * d, F32, "proj_gates")
            cpre = _conv_call(glu.reshape(b_, l_, D_CONV), *conv_vecs).reshape(b_ * l_, D_CONV)
            y = _merge_call(cpre, na.reshape(b_ * l_, D_ATT), sg, w_conv_l, w_na_l)
            z, h2 = _oproj_call(y.reshape(b_, l_, d), w_o_l, z, gate, n2g, shift2, scale2)
            return z, h2.reshape(b_ * l_, d)

        q_c, k_c, v_c = mixer_inputs(h_c, n_ctx, not last)
        q_l, k_l, v_l = mixer_inputs(h_l, t, True)
        bias = _na_bias_table(rpb[l])
        na_l = _na_call(q_l, k_l, v_l, k_c, v_c, bias)
        x, h2_l = mixer_tail(x, h_l, na_l, g1, sh2, sc2)
        if not last:
            na_c = _ctx_attn_call(q_c, k_c, v_c)
            cx, h2_c = mixer_tail(cx, h_c, na_c, cg1, csh2, csc2)

        w_pq_l = w_pq[l].astype(BF16)
        sk_l = sub_keys[l].reshape(2 * P_HEADS, N_KEYS, D_KEY_HALF).astype(BF16)
        w_down_l = w_down[l].astype(BF16)
        w_up_t_l = w_up[l].astype(BF16).T
        next_l = next_c = None
        if not last:
            (nsh1, nsc1, _, _, _, _), (ncsh1, ncsc1, _, _, _, _) = layer_mods(l + 1)
            next_l = (norm1_g[l + 1][None], nsh1, nsc1)
            next_c = (norm1_g[l + 1][None], ncsh1[:1], ncsc1[:1])
        x, h_l = _peer_block(x, h2_l, g2, w_pq_l, sk_l, w_down_l, w_up_t_l, next_l)
        if not last:
            cx, h_c = _peer_block(cx, h2_c, cg2[:1], w_pq_l, sk_l, w_down_l, w_up_t_l, next_c)
    return x
```

```python
import functools
import math

import numpy as np
import jax
import jax.numpy as jnp
from jax import lax
from jax.experimental import pallas as pl
from jax.experimental.pallas import tpu as pltpu

F32 = jnp.float32
BF16 = jnp.bfloat16

GRID_W = 64
D_CONV = 1024
CONV_W = 31
CONV_PAD = CONV_W // 2
NA_HEADS = 8
NA_DH = 128
D_ATT = NA_HEADS * NA_DH
NA_KH = 8
NA_KW = 16
P_HEADS = 8
N_KEYS = 128
D_KEY_HALF = 128
PK_TOPK = 16
EPS = 1e-6

LANES = 128
SUBLANES = 8
HALO = 16
NA_ROWS_PER_STEP = 4
NA_BAND_BLOCKS = 3
MASK_VALUE = -1e30
VMEM_LIMIT = 56 * 1024 * 1024

NT_DIMS = (((1,), (1,)), ((), ()))


def _params(sem, vmem=VMEM_LIMIT):
    return pltpu.CompilerParams(dimension_semantics=sem, vmem_limit_bytes=vmem)


def _sigmoid(x):
    return 1.0 / (1.0 + jnp.exp(-x))


def _mod_kernel(c_ref, w_ref, b_ref, o_ref):
    c = c_ref[...]
    s = (c * _sigmoid(c)).astype(BF16)
    o_ref[0] = jnp.dot(s, w_ref[0].astype(BF16), preferred_element_type=F32) + b_ref[0]


def _mod_call(cvec, w_mod, b_mod):
    nl, d, n = w_mod.shape
    r = cvec.shape[0]
    tn = 512
    return pl.pallas_call(
        _mod_kernel,
        grid=(nl, n // tn),
        in_specs=[pl.BlockSpec((r, d), lambda l, j: (0, 0)),
                  pl.BlockSpec((1, d, tn), lambda l, j: (l, 0, j)),
                  pl.BlockSpec((1, 1, tn), lambda l, j: (l, 0, j))],
        out_specs=pl.BlockSpec((1, r, tn), lambda l, j: (l, 0, j)),
        out_shape=jax.ShapeDtypeStruct((nl, r, n), F32),
        compiler_params=_params(("parallel", "parallel")),
        name="adaln_mod",
    )(cvec, w_mod, b_mod)


def _normmod(x, g, shift, scale):
    ms = jnp.mean(x * x, axis=-1, keepdims=True)
    n = x * lax.rsqrt(ms + EPS) * g
    return (n * (1.0 + scale) + shift).astype(BF16)


def _normmod_kernel(x_ref, g_ref, sh_ref, sc_ref, o_ref):
    o_ref[0] = _normmod(x_ref[0], g_ref[...], sh_ref[0], sc_ref[0])


def _normmod_call(x, g, shift, scale):
    b, l, d = x.shape
    tl = min(l, 512)
    return pl.pallas_call(
        _normmod_kernel,
        grid=(b, l // tl),
        in_specs=[pl.BlockSpec((1, tl, d), lambda i, j: (i, j, 0)),
                  pl.BlockSpec((1, d), lambda i, j: (0, 0)),
                  pl.BlockSpec((1, 1, d), lambda i, j: (i, 0, 0)),
                  pl.BlockSpec((1, 1, d), lambda i, j: (i, 0, 0))],
        out_specs=pl.BlockSpec((1, tl, d), lambda i, j: (i, j, 0)),
        out_shape=jax.ShapeDtypeStruct((b, l, d), BF16),
        compiler_params=_params(("parallel", "parallel")),
        name="normmod",
    )(x, g, shift, scale)


def _mm_glu_kernel(h_ref, wa_ref, wg_ref, o_ref):
    h = h_ref[...]
    a = jnp.dot(h, wa_ref[...], preferred_element_type=F32)
    g = jnp.dot(h, wg_ref[...], preferred_element_type=F32)
    o_ref[...] = (a * _sigmoid(g)).astype(o_ref.dtype)


def _mm_qkv_kernel(h_ref, w_ref, g_ref, o_ref, *, norm_tiles):
    y = jnp.dot(h_ref[...], w_ref[...], preferred_element_type=F32)
    j = pl.program_id(1)

    @pl.when(j < norm_tiles)
    def _():
        for k in range(y.shape[1] // LANES):
            sl = slice(k * LANES, (k + 1) * LANES)
            yk = y[:, sl]
            ms = jnp.mean(yk * yk, axis=-1, keepdims=True)
            o_ref[:, sl] = (yk * lax.rsqrt(ms + EPS) * g_ref[:, sl]).astype(o_ref.dtype)

    @pl.when(j >= norm_tiles)
    def _():
        o_ref[...] = y.astype(o_ref.dtype)


def _mm_sigmoid_kernel(h_ref, w_ref, o_ref):
    y = jnp.dot(h_ref[...], w_ref[...], preferred_element_type=F32)
    o_ref[...] = _sigmoid(y).astype(o_ref.dtype)


def _mm_call(kernel, h, ws, extras, n_out, out_dtype, name):
    t, k = h.shape
    tm = min(t, 1024)
    tn = 512
    in_specs = [pl.BlockSpec((tm, k), lambda i, j: (i, 0))]
    in_specs += [pl.BlockSpec((k, tn), lambda i, j: (0, j)) for _ in ws]
    in_specs += [pl.BlockSpec((1, tn), lambda i, j: (0, j)) for _ in extras]
    return pl.pallas_call(
        kernel,
        grid=(t // tm, n_out // tn),
        in_specs=in_specs,
        out_specs=pl.BlockSpec((tm, tn), lambda i, j: (i, j)),
        out_shape=jax.ShapeDtypeStruct((t, n_out), out_dtype),
        compiler_params=_params(("parallel", "parallel")),
        name=name,
    )(h, *ws, *extras)


def _softmax_pv(s_list, v_list):
    m = s_list[0].max(axis=-1, keepdims=True)
    for s in s_list[1:]:
        m = jnp.maximum(m, s.max(axis=-1, keepdims=True))
    num = None
    den = None
    for s, v in zip(s_list, v_list):
        p = jnp.exp(s - m)
        d = p.sum(axis=-1, keepdims=True)
        o = jnp.dot(p.astype(BF16), v, preferred_element_type=F32)
        num = o if num is None else num + o
        den = d if den is None else den + d
    return num / den


def _na_kernel(q_ref, k0_ref, k1_ref, k2_ref, v0_ref, v1_ref, v2_ref, kc_ref, vc_ref, bias_ref, o_ref):
    scale = NA_DH ** -0.5
    for h in range(NA_HEADS):
        sl = slice(h * NA_DH, (h + 1) * NA_DH)
        q = q_ref[0, :, sl]
        kb = jnp.concatenate([k0_ref[0, :, sl], k1_ref[0, :, sl], k2_ref[0, :, sl]], axis=0)
        vb = jnp.concatenate([v0_ref[0, :, sl], v1_ref[0, :, sl], v2_ref[0, :, sl]], axis=0)
        s_w = lax.dot_general(q, kb, NT_DIMS, preferred_element_type=F32) * scale + bias_ref[0, h]
        s_c = lax.dot_general(q, kc_ref[0, :, sl], NT_DIMS, preferred_element_type=F32) * scale
        o = _softmax_pv([s_w, s_c], [vb, vc_ref[0, :, sl]])
        o_ref[0, :, sl] = o.astype(o_ref.dtype)


def _na_bias_table(rpb_l):
    r4 = NA_ROWS_PER_STEP
    band_rows = NA_BAND_BLOCKS * r4
    n_dr, n_dc = 2 * NA_KH - 1, 2 * NA_KW - 1
    qc = np.arange(GRID_W)[:, None]
    kc = np.arange(GRID_W)[None, :]
    cs = np.clip(qc - NA_KW // 2, 0, GRID_W - NA_KW)
    col_ok = (kc >= cs) & (kc < cs + NA_KW)
    col_sel = (np.arange(n_dc)[:, None, None] == (kc - qc + NA_KW - 1)[None]) & col_ok[None]
    qr = np.arange(r4)[:, None]
    kr = np.arange(band_rows)[None, :]
    row_sel, row_ok = [], []
    for off, ks in ((0, np.zeros_like(qr)), (r4, qr), (band_rows - r4, np.full_like(qr, band_rows - NA_KH))):
        ok = (kr >= ks) & (kr < ks + NA_KH)
        row_ok.append(ok)
        row_sel.append((np.arange(n_dr)[None, None, :] == (kr - off - qr + NA_KH - 1)[:, :, None]) & ok[:, :, None])
    row_sel = jnp.asarray(np.stack(row_sel), F32)
    col_sel = jnp.asarray(col_sel, F32)
    bias = jnp.einsum("vqkr,hrc,cxy->vhqxky", row_sel, rpb_l, col_sel, precision=lax.Precision.HIGHEST)
    ok = np.stack(row_ok)[:, None, :, None, :, None] & col_ok[None, None, None, :, None, :]
    bias = jnp.where(ok, bias, MASK_VALUE)
    return bias.reshape(3, NA_HEADS, r4 * GRID_W, band_rows * GRID_W)


def _na_call(qkv, qkv_c, bias):
    b, t, _ = qkv.shape
    c = qkv_c.shape[1]
    tq = NA_ROWS_PER_STEP * GRID_W
    nb = t // tq
    assert nb >= NA_BAND_BLOCKS + 1 and NA_KH == 2 * NA_ROWS_PER_STEP

    def band(d, col):
        return pl.BlockSpec((1, tq, D_ATT), lambda i, j: (i, jnp.clip(j - 1, 0, nb - NA_BAND_BLOCKS) + d, col))

    def variant(i, j):
        return (jnp.where(j == 0, 0, jnp.where(j == nb - 1, 2, 1)), 0, 0, 0)

    return pl.pallas_call(
        _na_kernel,
        grid=(b, nb),
        in_specs=[pl.BlockSpec((1, tq, D_ATT), lambda i, j: (i, j, 0)),
                  band(0, 1), band(1, 1), band(2, 1), band(0, 2), band(1, 2), band(2, 2),
                  pl.BlockSpec((1, c, D_ATT), lambda i, j: (i, 0, 1)),
                  pl.BlockSpec((1, c, D_ATT), lambda i, j: (i, 0, 2)),
                  pl.BlockSpec((1, NA_HEADS, tq, NA_BAND_BLOCKS * tq), variant)],
        out_specs=pl.BlockSpec((1, tq, D_ATT), lambda i, j: (i, j, 0)),
        out_shape=jax.ShapeDtypeStruct((b, t, D_ATT), BF16),
        compiler_params=_params(("parallel", "arbitrary")),
        name="neighborhood_attention",
    )(qkv, qkv, qkv, qkv, qkv, qkv, qkv, qkv_c, qkv_c, bias)


def _ctx_attn_kernel(q_ref, k_ref, v_ref, o_ref):
    scale = NA_DH ** -0.5
    for h in range(NA_HEADS):
        sl = slice(h * NA_DH, (h + 1) * NA_DH)
        s = lax.dot_general(q_ref[0, :, sl], k_ref[0, :, sl], NT_DIMS, preferred_element_type=F32) * scale
        o_ref[0, :, sl] = _softmax_pv([s], [v_ref[0, :, sl]]).astype(o_ref.dtype)


def _ctx_attn_call(qkv):
    b, c, _ = qkv.shape
    return pl.pallas_call(
        _ctx_attn_kernel,
        grid=(b,),
        in_specs=[pl.BlockSpec((1, c, D_ATT), lambda i: (i, 0, 0)),
                  pl.BlockSpec((1, c, D_ATT), lambda i: (i, 0, 1)),
                  pl.BlockSpec((1, c, D_ATT), lambda i: (i, 0, 2))],
        out_specs=pl.BlockSpec((1, c, D_ATT), lambda i: (i, 0, 0)),
        out_shape=jax.ShapeDtypeStruct((b, c, D_ATT), BF16),
        compiler_params=_params(("parallel",)),
        name="context_attention",
    )(qkv, qkv, qkv)


def _conv_kernel(prev_ref, cur_ref, next_ref, w_ref, b_ref, g_ref, bb_ref, o_ref, buf_ref):
    i = pl.program_id(1)
    tl = cur_ref.shape[1]
    buf_ref[0, 0:HALO] = jnp.where(i > 0, prev_ref[0], 0.0)
    buf_ref[0, HALO:HALO + tl] = cur_ref[0]
    buf_ref[0, HALO + tl:2 * HALO + tl] = jnp.where(i < pl.num_programs(1) - 1, next_ref[0], 0.0)
    span = tl + 2 * HALO - SUBLANES
    for r in range(1, SUBLANES):
        buf_ref[r, 0:span] = buf_ref[0, r:r + span]
    rows = 32
    base = HALO - CONV_PAD
    for r0 in range(0, tl, rows):
        acc = jnp.zeros((rows, D_CONV), F32) + b_ref[...]
        for k in range(CONV_W):
            q, r = divmod(base + k, SUBLANES)
            start = r0 + q * SUBLANES
            acc = acc + buf_ref[r, start:start + rows, :] * w_ref[k:k + 1, :]
        mu = jnp.mean(acc, axis=-1, keepdims=True)
        cen = acc - mu
        var = jnp.mean(cen * cen, axis=-1, keepdims=True)
        y = cen * lax.rsqrt(var + EPS) * g_ref[...] + bb_ref[...]
        o_ref[0, r0:r0 + rows, :] = (y * _sigmoid(y)).astype(o_ref.dtype)


def _conv_call(glu, dw_w, dw_b, ln_g, ln_b):
    b, l, c = glu.shape
    tl = min(l, 128)
    hb = tl // HALO
    last_halo = l // HALO - 1
    vec = pl.BlockSpec((1, c), lambda i, j: (0, 0))
    return pl.pallas_call(
        _conv_kernel,
        grid=(b, l // tl),
        in_specs=[pl.BlockSpec((1, HALO, c), lambda i, j: (i, jnp.maximum(j * hb - 1, 0), 0)),
                  pl.BlockSpec((1, tl, c), lambda i, j: (i, j, 0)),
                  pl.BlockSpec((1, HALO, c), lambda i, j: (i, jnp.minimum((j + 1) * hb, last_halo), 0)),
                  pl.BlockSpec((CONV_W, c), lambda i, j: (0, 0)),
                  vec, vec, vec],
        out_specs=pl.BlockSpec((1, tl, c), lambda i, j: (i, j, 0)),
        out_shape=jax.ShapeDtypeStruct((b, l, c), BF16),
        scratch_shapes=[pltpu.VMEM((SUBLANES, tl + 2 * HALO, c), F32)],
        compiler_params=_params(("parallel", "arbitrary")),
        name="conformer_conv",
    )(glu, glu, glu, dw_w, dw_b, ln_g, ln_b)


def _merge_kernel(cp_ref, na_ref, sgc_ref, sgn_ref, wc_ref, wn_ref, y_ref):
    conv = jnp.dot(cp_ref[...], wc_ref[...], preferred_element_type=F32)
    na = jnp.dot(na_ref[...], wn_ref[...], preferred_element_type=F32)
    y_ref[...] = (sgc_ref[...] * conv + sgn_ref[...] * na).astype(y_ref.dtype)


def _merge_call(cpre, na, sg, w_conv_out, w_na_out):
    t, kc = cpre.shape
    d = w_conv_out.shape[1]
    tm = min(t, 512)
    return pl.pallas_call(
        _merge_kernel,
        grid=(t // tm,),
        in_specs=[pl.BlockSpec((tm, kc), lambda i: (i, 0)),
                  pl.BlockSpec((tm, kc), lambda i: (i, 0)),
                  pl.BlockSpec((tm, d), lambda i: (i, 0)),
                  pl.BlockSpec((tm, d), lambda i: (i, 1)),
                  pl.BlockSpec((kc, d), lambda i: (0, 0)),
                  pl.BlockSpec((kc, d), lambda i: (0, 0))],
        out_specs=pl.BlockSpec((tm, d), lambda i: (i, 0)),
        out_shape=jax.ShapeDtypeStruct((t, d), BF16),
        compiler_params=_params(("parallel",)),
        name="branch_merge",
    )(cpre, na, sg, sg, w_conv_out, w_na_out)


def _oproj_kernel(y_ref, w_ref, x_ref, g_ref, ng_ref, nsh_ref, nsc_ref, o_ref, h_ref):
    out = x_ref[0] + g_ref[0] * jnp.dot(y_ref[0], w_ref[...], preferred_element_type=F32)
    o_ref[0] = out
    h_ref[0] = _normmod(out, ng_ref[...], nsh_ref[0], nsc_ref[0])


def _oproj_call(y, w_o, x, gate, norm_g, shift, scale):
    b, l, d = x.shape
    tm = min(l, 512)
    tile = pl.BlockSpec((1, tm, d), lambda i, j: (i, j, 0))
    vec = pl.BlockSpec((1, 1, d), lambda i, j: (i, 0, 0))
    return pl.pallas_call(
        _oproj_kernel,
        grid=(b, l // tm),
        in_specs=[tile,
                  pl.BlockSpec((d, d), lambda i, j: (0, 0)),
                  tile, vec,
                  pl.BlockSpec((1, d), lambda i, j: (0, 0)),
                  vec, vec],
        out_specs=[tile, tile],
        out_shape=[jax.ShapeDtypeStruct((b, l, d), F32), jax.ShapeDtypeStruct((b, l, d), BF16)],
        compiler_params=_params(("parallel", "parallel")),
        name="out_proj_residual",
    )(y, w_o, x, gate, norm_g, shift, scale)


def _peer_scores_kernel(h_ref, w_ref, sk_ref, o_ref):
    q = jnp.dot(h_ref[...], w_ref[...], preferred_element_type=F32)
    for g in range(2 * P_HEADS):
        qg = q[:, g * D_KEY_HALF:(g + 1) * D_KEY_HALF]
        ms = jnp.mean(qg * qg, axis=-1, keepdims=True)
        qn = (qg * lax.rsqrt(ms + EPS)).astype(BF16)
        o_ref[g] = lax.dot_general(sk_ref[g], qn, NT_DIMS, preferred_element_type=F32)


def _peer_scores_call(h, w_pq, sub_keys):
    t, d = h.shape
    ng = 2 * P_HEADS
    tm = min(t, 256)
    return pl.pallas_call(
        _peer_scores_kernel,
        grid=(t // tm,),
        in_specs=[pl.BlockSpec((tm, d), lambda i: (i, 0)),
                  pl.BlockSpec((d, ng * D_KEY_HALF), lambda i: (0, 0)),
                  pl.BlockSpec((ng, N_KEYS, D_KEY_HALF), lambda i: (0, 0, 0))],
        out_specs=pl.BlockSpec((ng, N_KEYS, tm), lambda i: (0, 0, i)),
        out_shape=jax.ShapeDtypeStruct((ng, N_KEYS, t), F32),
        compiler_params=_params(("parallel",)),
        name="peer_scores",
    )(h, w_pq, sub_keys)


def _top16_rows(s, iota, tie_break, want_rank):
    rank = jnp.full(s.shape, float(PK_TOPK), F32) if want_rank else None
    vals = []
    for k in range(PK_TOPK):
        m = jnp.max(s, axis=0, keepdims=True)
        if tie_break:
            idx = jnp.min(jnp.where(s == m, iota, float(N_KEYS)), axis=0, keepdims=True)
            sel = iota == idx
        else:
            sel = s == m
        if want_rank:
            rank = jnp.where(sel, float(k), rank)
        s = jnp.where(sel, -jnp.inf, s)
        vals.append(m)
    taken = jnp.sum(jnp.where(s == -jnp.inf, 1.0, 0.0), axis=0, keepdims=True)
    return vals, rank, taken


def _peer_topk_kernel(s_ref, rank2_ref, cnt_ref, e2_ref, e1_ref):
    width = s_ref.shape[2]
    n_cand = 80
    r = lax.broadcasted_iota(jnp.int32, (n_cand, width), 0)
    ci = jnp.where(r < 16, 0, jnp.where(r < 72, 1 + ((r - 16) >> 3), r - 64))
    cj = jnp.where(r < 16, r, jnp.where(r < 72, (r - 16) & 7, 0))
    valid = (ci + 1) * (cj + 1) <= PK_TOPK
    pos = (ci * PK_TOPK + cj).astype(F32)
    iota = lax.broadcasted_iota(jnp.int32, (N_KEYS, width), 0).astype(F32)

    def select(ls, tie_break):
        s1 = s_ref[0, :, ls]
        s2 = s_ref[1, :, ls]
        v1, rank1, taken1 = _top16_rows(s1, iota, tie_break, tie_break)
        v2, rank2, taken2 = _top16_rows(s2, iota, tie_break, True)
        v2_all = jnp.concatenate(v2, axis=0)
        blocks = [v1[0] + v2_all]
        for i in range(1, 8):
            blocks.append(v1[i] + v2_all[0:8])
        blocks.append(jnp.concatenate(v1[8:], axis=0) + v2[0])
        cand = jnp.where(valid, jnp.concatenate(blocks, axis=0), -jnp.inf)
        selm = jnp.zeros((n_cand, width), F32)
        top = None
        z = None
        for k in range(PK_TOPK):
            m = jnp.max(cand, axis=0, keepdims=True)
            if tie_break:
                pmin = jnp.min(jnp.where(cand == m, pos, 1e9), axis=0, keepdims=True)
                sel = pos == pmin
            else:
                sel = cand == m
            selm = jnp.where(sel, 1.0, selm)
            cand = jnp.where(sel, -jnp.inf, cand)
            if k == 0:
                top = m
                z = jnp.ones_like(m)
            else:
                z = z + jnp.exp(m - top)
        counts = [jnp.sum(selm[0:16], axis=0, keepdims=True)]
        for i in range(1, 8):
            counts.append(jnp.sum(selm[8 + 8 * i:16 + 8 * i], axis=0, keepdims=True))
        for q in range(8):
            counts.append(selm[72 + q:73 + q])
        cnt = jnp.zeros((N_KEYS, width), F32)
        for i in range(PK_TOPK):
            hit = (rank1 == float(i)) if tie_break else (s1 == v1[i])
            cnt = jnp.where(hit, counts[i], cnt)
        rank2_ref[0, :, ls] = rank2.astype(rank2_ref.dtype)
        cnt_ref[0, :, ls] = cnt
        e2_ref[0, :, ls] = jnp.exp(s2 - v2[0]).astype(e2_ref.dtype)
        e1_ref[0, :, ls] = jnp.exp(s1 - v1[0]) / z
        taken = counts[0]
        for cnt_i in counts[1:]:
            taken = taken + cnt_i
        return taken1 + taken2 + taken

    ls = slice(None)
    taken = select(ls, tie_break=False)

    @pl.when(jnp.max(taken) > 3.0 * PK_TOPK)
    def _():
        select(ls, tie_break=True)


def _peer_topk_call(scores):
    ng, nk, t = scores.shape
    tt = min(t, 512)
    out_f = jax.ShapeDtypeStruct((P_HEADS, nk, t), F32)
    out_b = jax.ShapeDtypeStruct((P_HEADS, nk, t), BF16)
    ospec = pl.BlockSpec((1, nk, tt), lambda i, h: (h, 0, i))
    return pl.pallas_call(
        _peer_topk_kernel,
        grid=(t // tt, P_HEADS),
        in_specs=[pl.BlockSpec((2, nk, tt), lambda i, h: (h, 0, i))],
        out_specs=[ospec, ospec, ospec, ospec],
        out_shape=[out_b, out_f, out_b, out_f],
        compiler_params=_params(("parallel", "parallel")),
        name="peer_topk",
    )(scores)


BF16_ROWS = 16


def _peer_dense_kernel(h_ref, rank2_ref, cnt_ref, e2_ref, e1_ref, wd_ref, wu_ref, x_ref, g_ref, *rest, emit_next):
    if emit_next:
        ng_ref, nsh_ref, nsc_ref, o_ref, hn_ref, acc_ref, m_ref = rest
    else:
        o_ref, acc_ref, m_ref = rest
    j = pl.program_id(1)
    ce, tt = m_ref.shape
    groups = N_KEYS // BF16_ROWS

    @pl.when(j == 0)
    def _():
        acc_ref[...] = jnp.zeros_like(acc_ref)

    a_t = lax.dot_general(wd_ref[...], h_ref[...], NT_DIMS, preferred_element_type=F32)
    for ai in range(ce // N_KEYS):
        rows = slice(ai * N_KEYS, (ai + 1) * N_KEYS)
        g = None
        for h in range(P_HEADS):
            cn = jnp.broadcast_to(cnt_ref[h, ai:ai + 1, :], (BF16_ROWS, tt)).astype(BF16)[None]
            e1 = jnp.broadcast_to(e1_ref[h, ai:ai + 1, :], (BF16_ROWS, tt)).astype(BF16)[None]
            r2 = rank2_ref[h].reshape(groups, BF16_ROWS, tt)
            e2 = e2_ref[h].reshape(groups, BF16_ROWS, tt)
            term = jnp.where(r2 < cn, e2 * e1, jnp.zeros_like(e2))
            g = term if g is None else g + term
        at = a_t[rows, :]
        act = (at * (lax.erf(at * (1.0 / math.sqrt(2.0))) + 1.0) * 0.5).astype(BF16)
        m_ref[rows, :] = g.reshape(N_KEYS, tt) * act
    acc_ref[...] += jnp.dot(wu_ref[...], m_ref[...], preferred_element_type=F32)

    @pl.when(j == pl.num_programs(1) - 1)
    def _():
        out = x_ref[...] + g_ref[0] * acc_ref[...].T
        o_ref[...] = out
        if emit_next:
            hn_ref[...] = _normmod(out, ng_ref[...], nsh_ref[0], nsc_ref[0])


def _peer_dense_call(h, rank2, cnt, e2, e1, w_down, w_up_t, x, gate, next_norm):
    t, d = h.shape
    e = w_down.shape[0]
    tt = 512 if t % 512 == 0 else 256
    ce = 1024
    per_step = ce // N_KEYS
    tiles_per_gate = t // (gate.shape[0] * tt)
    cnt = cnt.reshape(P_HEADS, e // ce, per_step, t)
    e1 = e1.reshape(P_HEADS, e // ce, per_step, t)
    once = dict(pipeline_mode=pl.Buffered(1))
    sel_spec = pl.BlockSpec((P_HEADS, N_KEYS, tt), lambda i, j: (0, 0, i), **once)
    row_spec = pl.BlockSpec((P_HEADS, None, per_step, tt), lambda i, j: (0, j, 0, i))
    tile = pl.BlockSpec((tt, d), lambda i, j: (i, 0))
    vec = pl.BlockSpec((1, 1, d), lambda i, j: (i // tiles_per_gate, 0, 0))
    in_specs = [pl.BlockSpec((tt, d), lambda i, j: (i, 0), **once),
                sel_spec, row_spec, sel_spec, row_spec,
                pl.BlockSpec((ce, d), lambda i, j: (j, 0)),
                pl.BlockSpec((d, ce), lambda i, j: (0, j)),
                pl.BlockSpec((tt, d), lambda i, j: (i, 0), **once),
                vec]
    args = [h, rank2, cnt, e2, e1, w_down, w_up_t, x, gate]
    out_specs, out_shape = tile, jax.ShapeDtypeStruct((t, d), F32)
    if next_norm is not None:
        in_specs += [pl.BlockSpec((1, d), lambda i, j: (0, 0)), vec, vec]
        args += list(next_norm)
        out_specs, out_shape = [tile, tile], [out_shape, jax.ShapeDtypeStruct((t, d), BF16)]
    res = pl.pallas_call(
        functools.partial(_peer_dense_kernel, emit_next=next_norm is not None),
        grid=(t // tt, e // ce),
        in_specs=in_specs,
        out_specs=out_specs,
        out_shape=out_shape,
        scratch_shapes=[pltpu.VMEM((d, tt), F32), pltpu.VMEM((ce, tt), BF16)],
        compiler_params=_params(("parallel", "arbitrary"), 60 * 1024 * 1024),
        name="peer_dense",
    )(*args)
    return res if next_norm is not None else (res, None)


def _peer_block(x, h, gate, w_pq, sub_keys, w_down, w_up_t, next_norm):
    b, l, d = x.shape
    scores = _peer_scores_call(h, w_pq, sub_keys)
    rank2, cnt, e2, e1 = _peer_topk_call(scores)
    out, h_next = _peer_dense_call(h, rank2, cnt, e2, e1, w_down, w_up_t, x.reshape(b * l, d), gate, next_norm)
    return out.reshape(b, l, d), h_next


def kernel(x, c, ctx, c_ctx, w_mod, b_mod, norm1_g, w_in, dw_w, dw_b, cln_g, cln_b, w_conv_out,
           qn_g, kn_g, rpb, w_na_out, w_o, norm2_g, w_pq, sub_keys, w_down, w_up):
    bsz, t, d = x.shape
    n_ctx = ctx.shape[1]
    depth = w_mod.shape[0]
    off_q = 2 * D_CONV
    off_gate = off_q + 3 * D_ATT

    n_rows_mod = 8
    cvec = jnp.concatenate([c, c_ctx[None], jnp.zeros((n_rows_mod - bsz - 1, d), F32)], axis=0)
    mod = _mod_call(cvec, w_mod, b_mod[:, None, :])

    def layer_mods(l):
        mod_l = mod[l, :bsz].reshape(bsz, 1, 6, d)
        mod_c = jnp.broadcast_to(mod[l, bsz].reshape(1, 1, 6, d), (bsz, 1, 6, d))
        return [mod_l[:, :, i] for i in range(6)], [mod_c[:, :, i] for i in range(6)]

    cx = ctx
    (sh1, sc1, _, _, _, _), (csh1, csc1, _, _, _, _) = layer_mods(0)
    h_l = _normmod_call(x, norm1_g[0][None], sh1, sc1).reshape(bsz * t, d)
    h_c = _normmod_call(cx, norm1_g[0][None], csh1, csc1).reshape(bsz * n_ctx, d)
    for l in range(depth):
        last = l == depth - 1
        (_, _, g1, sh2, sc2, g2), (_, _, cg1, csh2, csc2, cg2) = layer_mods(l)

        w_in_l = w_in[l]
        w_glu_a = w_in_l[:, :D_CONV].astype(BF16)
        w_glu_g = w_in_l[:, D_CONV:off_q].astype(BF16)
        w_qkv = w_in_l[:, off_q:off_gate].astype(BF16)
        w_gate = w_in_l[:, off_gate:].astype(BF16)
        g_qkv = jnp.concatenate([jnp.tile(qn_g[l], NA_HEADS), jnp.tile(kn_g[l], NA_HEADS), jnp.ones((D_ATT,), F32)])[None]
        w_conv_l = w_conv_out[l].astype(BF16)
        w_na_l = w_na_out[l].astype(BF16)
        w_o_l = w_o[l].astype(BF16)
        n2g = norm2_g[l][None]
        conv_vecs = (dw_w[l], dw_b[l][None], cln_g[l][None], cln_b[l][None])

        def mixer_inputs(h, l_):
            qkv = _mm_call(functools.partial(_mm_qkv_kernel, norm_tiles=2 * D_ATT // 512), h, [w_qkv], [g_qkv],
                           3 * D_ATT, BF16, "proj_qkv")
            return qkv.reshape(h.shape[0] // l_, l_, 3 * D_ATT)

        def mixer_tail(z, h, na, gate, shift2, scale2):
            b_, l_, _ = z.shape
            glu = _mm_call(_mm_glu_kernel, h, [w_glu_a, w_glu_g], [], D_CONV, F32, "proj_glu")
            sg = _mm_call(_mm_sigmoid_kernel, h, [w_gate], [], 2 * d, F32, "proj_gates")
            cpre = _conv_call(glu.reshape(b_, l_, D_CONV), *conv_vecs).reshape(b_ * l_, D_CONV)
            y = _merge_call(cpre, na.reshape(b_ * l_, D_ATT), sg, w_conv_l, w_na_l)
            z, h2 = _oproj_call(y.reshape(b_, l_, d), w_o_l, z, gate, n2g, shift2, scale2)
            return z, h2.reshape(b_ * l_, d)

        qkv_c = mixer_inputs(h_c, n_ctx)
        qkv_l = mixer_inputs(h_l, t)
        bias = _na_bias_table(rpb[l])
        na_l = _na_call(qkv_l, qkv_c, bias)
        x, h2_l = mixer_tail(x, h_l, na_l, g1, sh2, sc2)
        if not last:
            na_c = _ctx_attn_call(qkv_c)
            cx, h2_c = mixer_tail(cx, h_c, na_c, cg1, csh2, csc2)

        w_pq_l = w_pq[l].astype(BF16)
        sk_l = sub_keys[l].reshape(2 * P_HEADS, N_KEYS, D_KEY_HALF).astype(BF16)
        w_down_l = w_down[l].astype(BF16)
        w_up_t_l = w_up[l].astype(BF16).T
        next_l = next_c = None
        if not last:
            (nsh1, nsc1, _, _, _, _), (ncsh1, ncsc1, _, _, _, _) = layer_mods(l + 1)
            next_l = (norm1_g[l + 1][None], nsh1, nsc1)
            next_c = (norm1_g[l + 1][None], ncsh1[:1], ncsc1[:1])
        x, h_l = _peer_block(x, h2_l, g2, w_pq_l, sk_l, w_down_l, w_up_t_l, next_l)
        if not last:
            cx, h_c = _peer_block(cx, h2_c, cg2[:1], w_pq_l, sk_l, w_down_l, w_up_t_l, next_c)
    return x
```
